```python
import math
import jax, jax.numpy as jnp
from jax import lax
import numpy as np

D_MODEL = 2048
BATCH = 2
SEQ = 16384
DEPTH = 2

CHUNK = 64
LEFT_CHUNKS = 8
BAND = LEFT_CHUNKS + 1

MIX_WIDTH = D_MODEL
ATTN_WIDTH = MIX_WIDTH // 2
POOL_WIDTH = MIX_WIDTH - ATTN_WIDTH
HEAD_DIM = 128
N_HEADS = ATTN_WIDTH // HEAD_DIM
REL_CLIP = 128
POOL_WINDOWS = (2, 4, 8, 16)
N_POOL_GROUPS = len(POOL_WINDOWS)
POOL_GROUP = POOL_WIDTH // N_POOL_GROUPS
IN_WIDTH = 3 * ATTN_WIDTH + POOL_WIDTH

D_FF = 5632
CONV_WIDTH = 3
NORM_EPS = 1e-6

kernel_name = "hybrid_chunked_attn_multiscale_pool_convffn"


def rms_norm(x, g):
    xf = x.astype(jnp.float32)
    y = xf * lax.rsqrt(jnp.mean(xf * xf, axis=-1, keepdims=True) + NORM_EPS)
    return (y * g.astype(jnp.float32)).astype(x.dtype)


def _band_bias_index():
    q_pos = np.arange(CHUNK) + LEFT_CHUNKS * CHUNK
    k_pos = np.arange(BAND * CHUNK)
    rel = np.clip(q_pos[:, None] - k_pos[None, :], -REL_CLIP, REL_CLIP)
    return (rel + REL_CLIP).astype(np.int32)


def chunked_attention(q, k, v, rel_bias):
    B, S, H, Dh = q.shape
    N = S // CHUNK
    qc = q.reshape(B, N, CHUNK, H, Dh)
    pad = ((0, 0), (LEFT_CHUNKS * CHUNK, 0), (0, 0), (0, 0))
    kc = jnp.pad(k, pad).reshape(B, N + LEFT_CHUNKS, CHUNK, H, Dh)
    vc = jnp.pad(v, pad).reshape(B, N + LEFT_CHUNKS, CHUNK, H, Dh)
    scale = 1.0 / math.sqrt(Dh)
    scores = jnp.concatenate(
        [jnp.einsum('bnqhd,bnkhd->bnhqk', qc, kc[:, j:j + N],
                    preferred_element_type=jnp.float32) for j in range(BAND)],
        axis=-1) * scale
    bias = rel_bias.astype(jnp.float32)[:, _band_bias_index()]
    scores = scores + bias[None, None]
    key_chunk = np.repeat(np.arange(BAND), CHUNK)
    valid = (jnp.arange(N)[:, None] + key_chunk[None, :]) >= LEFT_CHUNKS
    scores = jnp.where(valid[None, :, None, None, :], scores, jnp.float32(-1e30))
    probs = jax.nn.softmax(scores, axis=-1).astype(v.dtype)
    out = 0.0
    for j in range(BAND):
        out = out + jnp.einsum('bnhqk,bnkhd->bnqhd',
                               probs[..., j * CHUNK:(j + 1) * CHUNK], vc[:, j:j + N])
    return out.reshape(B, S, H, Dh)


def multiscale_pool(u, pool_w, pool_scale):
    B, S, _ = u.shape
    uf = u.astype(jnp.float32)
    cs = jnp.cumsum(uf, axis=1)
    pos_count = jnp.arange(1, S + 1, dtype=jnp.float32)
    groups = []
    for g, w in enumerate(POOL_WINDOWS):
        sl = slice(g * POOL_GROUP, (g + 1) * POOL_GROUP)
        csg = cs[..., sl]
        shifted = jnp.pad(csg, ((0, 0), (w, 0), (0, 0)))[:, :S]
        count = jnp.minimum(pos_count, jnp.float32(w))[None, :, None]
        groups.append((csg - shifted) / count - uf[..., sl])
    pooled = jnp.stack(groups, axis=2).astype(u.dtype)
    mixed = jnp.einsum('bsgc,gcd->bsgd', pooled, pool_w).reshape(B, S, POOL_WIDTH)
    return mixed * pool_scale


def causal_dwconv(u, conv_w, conv_b):
    S = u.shape[1]
    up = jnp.pad(u, ((0, 0), (CONV_WIDTH - 1, 0), (0, 0)))
    out = conv_b
    for t in range(CONV_WIDTH):
        out = out + up[:, t:t + S] * conv_w[t]
    return out


def setup_inputs(seed: int = 0) -> dict:
    key = jax.random.key(seed)
    ks = jax.random.split(key, 16)
    f32 = jnp.float32
    nrm = lambda k, shape, s: jax.random.normal(k, shape, f32) * s
    gain = lambda k, n: 1.0 + 0.02 * jax.random.normal(k, (DEPTH, n), f32)
    return {
        "x": nrm(ks[0], (BATCH, SEQ, D_MODEL), 1.0),
        "pre_mix_g": gain(ks[1], D_MODEL),
        "w_in": nrm(ks[2], (DEPTH, D_MODEL, IN_WIDTH), D_MODEL ** -0.5),
        "rel_bias": nrm(ks[3], (DEPTH, N_HEADS, 2 * REL_CLIP + 1), 0.1),
        "pool_w": nrm(ks[4], (DEPTH, N_POOL_GROUPS, POOL_GROUP, POOL_GROUP), POOL_GROUP ** -0.5),
        "pool_scale": gain(ks[5], POOL_WIDTH),
        "w_o": nrm(ks[6], (DEPTH, MIX_WIDTH, D_MODEL), MIX_WIDTH ** -0.5),
        "post_mix_g": gain(ks[7], D_MODEL),
        "pre_ffn_g": gain(ks[8], D_MODEL),
        "w_up": nrm(ks[9], (DEPTH, D_MODEL, 2 * D_FF), D_MODEL ** -0.5),
        "conv_w": nrm(ks[10], (DEPTH, CONV_WIDTH, 2 * D_FF), CONV_WIDTH ** -0.5),
        "conv_b": nrm(ks[11], (DEPTH, 2 * D_FF), 0.01),
        "w_down": nrm(ks[12], (DEPTH, D_FF, D_MODEL), D_FF ** -0.5),
        "post_ffn_g": gain(ks[13], D_MODEL),
    }


def reference(x, pre_mix_g, w_in, rel_bias, pool_w, pool_scale, w_o, post_mix_g,
              pre_ffn_g, w_up, conv_w, conv_b, w_down, post_ffn_g):
    B, S, _ = x.shape
    for l in range(DEPTH):
        h = rms_norm(x, pre_mix_g[l])
        z = h @ w_in[l]
        q = z[..., :ATTN_WIDTH].reshape(B, S, N_HEADS, HEAD_DIM)
        k = z[..., ATTN_WIDTH:2 * ATTN_WIDTH].reshape(B, S, N_HEADS, HEAD_DIM)
        v = z[..., 2 * ATTN_WIDTH:3 * ATTN_WIDTH].reshape(B, S, N_HEADS, HEAD_DIM)
        u = z[..., 3 * ATTN_WIDTH:]
        a = chunked_attention(q, k, v, rel_bias[l]).reshape(B, S, ATTN_WIDTH)
        p = multiscale_pool(u, pool_w[l], pool_scale[l])
        m = jnp.concatenate([a, p], axis=-1) @ w_o[l]
        x = x + rms_norm(m, post_mix_g[l])
        h = rms_norm(x, pre_ffn_g[l])
        up = causal_dwconv(h @ w_up[l], conv_w[l], conv_b[l])
        y = (jax.nn.gelu(up[..., :D_FF], approximate=True) * up[..., D_FF:]) @ w_down[l]
        x = x + rms_norm(y, post_ffn_g[l])
    return x
```

```python
import functools
import math

import jax
import jax.numpy as jnp
import numpy as np
from jax import lax
from jax.experimental import pallas as pl
from jax.experimental.pallas import tpu as pltpu

F32 = jnp.float32
BF16 = jnp.bfloat16

D_MODEL = 2048
CHUNK = 64
LEFT_CHUNKS = 8
ATTN_WIDTH = 1024
POOL_WIDTH = 1024
HEAD_DIM = 128
N_HEADS = ATTN_WIDTH // HEAD_DIM
REL_CLIP = 128
POOL_WINDOWS = (2, 4, 8, 16)
POOL_GROUP = POOL_WIDTH // len(POOL_WINDOWS)
IN_WIDTH = 3 * ATTN_WIDTH + POOL_WIDTH
D_FF = 5632
CONV_WIDTH = 3
NORM_EPS = 1e-6
MASK_VALUE = -1e30

V7X_VMEM_BYTES = 64 * 1024 * 1024
VMEM_LIMIT_BYTES = V7X_VMEM_BYTES - 6 * 1024 * 1024

IN_TM = 512
IN_TN = 1024
ATT_BQ = 2 * CHUNK
ATT_NKB = LEFT_CHUNKS * CHUNK // ATT_BQ + 1
ATT_BK = ATT_NKB * ATT_BQ
MIX_TM = 512
POOL_HALO = 16
FFN_TM = 512
FFN_TF = 512
CONV_HALO = 8


def _rms_scale(x):
    return lax.rsqrt(jnp.mean(x * x, axis=-1, keepdims=True) + NORM_EPS)


def _inproj_kernel(x_ref, g_ref, w_ref, qkv_ref, u_ref, h_ref):
    j = pl.program_id(1)

    @pl.when(j == 0)
    def _():
        x = x_ref[...]
        h_ref[...] = ((x * _rms_scale(x)) * g_ref[...]).astype(BF16)

    z = jnp.dot(h_ref[...], w_ref[...], preferred_element_type=F32)

    @pl.when(j == 0)
    def _():
        qkv_ref[...] = (z * (1.0 / math.sqrt(HEAD_DIM))).astype(BF16)

    @pl.when(jnp.logical_and(j > 0, j < 3))
    def _():
        qkv_ref[...] = z.astype(BF16)

    @pl.when(j == 3)
    def _():
        u_ref[...] = z


def _inproj(x, g, w):
    t = x.shape[0]
    grid = (t // IN_TM, IN_WIDTH // IN_TN)
    return pl.pallas_call(
        _inproj_kernel,
        grid=grid,
        in_specs=[
            pl.BlockSpec((IN_TM, D_MODEL), lambda i, j: (i, 0)),
            pl.BlockSpec((1, D_MODEL), lambda i, j: (0, 0)),
            pl.BlockSpec((D_MODEL, IN_TN), lambda i, j: (0, j)),
        ],
        out_specs=[
            pl.BlockSpec((IN_TM, IN_TN), lambda i, j: (i, jnp.minimum(j, 2))),
            pl.BlockSpec((IN_TM, POOL_WIDTH), lambda i, j: (i, 0)),
        ],
        out_shape=[
            jax.ShapeDtypeStruct((t, 3 * ATTN_WIDTH), BF16),
            jax.ShapeDtypeStruct((t, POOL_WIDTH), F32),
        ],
        scratch_shapes=[pltpu.VMEM((IN_TM, D_MODEL), BF16)],
        compiler_params=pltpu.CompilerParams(
            dimension_semantics=("arbitrary", "arbitrary"),
            vmem_limit_bytes=VMEM_LIMIT_BYTES),
        name="inproj",
    )(x, g, w)


def _band_bias(rel_bias):
    r = np.arange(ATT_BQ)
    c = np.arange(ATT_BK)
    q_chunk, q_in = r // CHUNK, r % CHUNK
    k_chunk, k_in = c // CHUNK, c % CHUNK
    band = k_chunk[None, :] - q_chunk[:, None]
    in_band = (band >= 0) & (band <= LEFT_CHUNKS)
    q_pos = q_in[:, None] + LEFT_CHUNKS * CHUNK
    k_pos = band * CHUNK + k_in[None, :]
    idx = np.clip(q_pos - k_pos, -REL_CLIP, REL_CLIP) + REL_CLIP
    idx = np.where(in_band, idx, 0).astype(np.int32)
    bias = rel_bias.astype(F32)[:, idx]
    return jnp.where(jnp.asarray(in_band)[None], bias, F32(MASK_VALUE))


def _attn_kernel(blocks_per_seq, q_ref, *refs):
    k_refs = refs[:ATT_NKB]
    v_refs = refs[ATT_NKB:2 * ATT_NKB]
    bias_ref, o_ref, kcat_ref, vcat_ref = refs[2 * ATT_NKB:]
    i_in_seq = pl.program_id(0) % blocks_per_seq

    for d in range(ATT_NKB):
        kcat_ref[d * ATT_BQ:(d + 1) * ATT_BQ, :] = k_refs[d][...]
        vcat_ref[d * ATT_BQ:(d + 1) * ATT_BQ, :] = v_refs[d][...]

    col_block = lax.broadcasted_iota(jnp.int32, (ATT_BQ, ATT_BK), 1) // ATT_BQ
    seq_valid = (col_block + i_in_seq) >= (ATT_NKB - 1)

    for h in range(N_HEADS):
        hs = slice(h * HEAD_DIM, (h + 1) * HEAD_DIM)
        s = lax.dot_general(q_ref[:, hs], kcat_ref[:, hs], (((1,), (1,)), ((), ())),
                            preferred_element_type=F32)
        s = jnp.where(seq_valid, s + bias_ref[h], MASK_VALUE)
        m = jnp.max(s, axis=-1, keepdims=True)
        p = jnp.exp(s - m)
        l = jnp.sum(p, axis=-1, keepdims=True)
        o = jnp.dot(p.astype(BF16), vcat_ref[:, hs], preferred_element_type=F32)
        o_ref[:, hs] = (o / l).astype(BF16)


def _attention(qkv, bias, seq_len):
    t = qkv.shape[0]
    blocks_per_seq = seq_len // ATT_BQ

    def kv_map(d, col):
        def index_map(i):
            first = (i // blocks_per_seq) * blocks_per_seq
            return (jnp.maximum(i - (ATT_NKB - 1) + d, first), col)
        return index_map

    kv_block = (ATT_BQ, ATTN_WIDTH)
    in_specs = [pl.BlockSpec(kv_block, lambda i: (i, 0))]
    in_specs += [pl.BlockSpec(kv_block, kv_map(d, 1)) for d in range(ATT_NKB)]
    in_specs += [pl.BlockSpec(kv_block, kv_map(d, 2)) for d in range(ATT_NKB)]
    in_specs += [pl.BlockSpec((N_HEADS, ATT_BQ, ATT_BK), lambda i: (0, 0, 0))]
    return pl.pallas_call(
        functools.partial(_attn_kernel, blocks_per_seq),
        grid=(t // ATT_BQ,),
        in_specs=in_specs,
        out_specs=pl.BlockSpec(kv_block, lambda i: (i, 0)),
        out_shape=jax.ShapeDtypeStruct((t, ATTN_WIDTH), BF16),
        scratch_shapes=[pltpu.VMEM((ATT_BK, ATTN_WIDTH), BF16),
                        pltpu.VMEM((ATT_BK, ATTN_WIDTH), BF16)],
        compiler_params=pltpu.CompilerParams(
            dimension_semantics=("arbitrary",),
            vmem_limit_bytes=VMEM_LIMIT_BYTES),
        name="attention",
    )(qkv, *([qkv] * (2 * ATT_NKB)), bias)


def _mixout_kernel(tiles_per_seq, x_ref, a_ref, u_ref, uh_ref, pw_ref, ps_ref, woa_ref, wop_ref,
                   g_ref, o_ref, ebuf_ref, p_ref):
    tm = MIX_TM
    i_in_seq = pl.program_id(0) % tiles_per_seq
    ebuf_ref[0:POOL_HALO, :] = jnp.where(i_in_seq == 0, 0.0, uh_ref[...])
    ebuf_ref[POOL_HALO:, :] = u_ref[...]

    pos = i_in_seq * tm + lax.broadcasted_iota(jnp.int32, (tm, POOL_GROUP), 0)
    for g, w in enumerate(POOL_WINDOWS):
        cs = slice(g * POOL_GROUP, (g + 1) * POOL_GROUP)
        tok = ebuf_ref[POOL_HALO:, cs]
        wsum = tok
        for k in range(1, w):
            wsum = wsum + ebuf_ref[POOL_HALO - k:POOL_HALO - k + tm, cs]
        count = jnp.minimum(pos + 1, w).astype(F32)
        pooled = (wsum / count - tok).astype(BF16)
        mixed = jnp.dot(pooled, pw_ref[g], preferred_element_type=F32)
        p_ref[:, cs] = (mixed * ps_ref[:, cs]).astype(BF16)

    m = jnp.dot(a_ref[...], woa_ref[...], preferred_element_type=F32)
    m = m + jnp.dot(p_ref[...], wop_ref[...], preferred_element_type=F32)
    o_ref[...] = x_ref[...] + (m * _rms_scale(m)) * g_ref[...]


def _mixout(x, a, u, pool_w, pool_scale, w_o, g, seq_len):
    t = x.shape[0]
    tm = MIX_TM
    tiles_per_seq = seq_len // tm
    halo_blocks = tm // POOL_HALO
    return pl.pallas_call(
        functools.partial(_mixout_kernel, tiles_per_seq),
        grid=(t // tm,),
        in_specs=[
            pl.BlockSpec((tm, D_MODEL), lambda i: (i, 0)),
            pl.BlockSpec((tm, ATTN_WIDTH), lambda i: (i, 0)),
            pl.BlockSpec((tm, POOL_WIDTH), lambda i: (i, 0)),
            pl.BlockSpec((POOL_HALO, POOL_WIDTH), lambda i: (jnp.maximum(i * halo_blocks - 1, 0), 0)),
            pl.BlockSpec((len(POOL_WINDOWS), POOL_GROUP, POOL_GROUP), lambda i: (0, 0, 0)),
            pl.BlockSpec((1, POOL_WIDTH), lambda i: (0, 0)),
            pl.BlockSpec((ATTN_WIDTH, D_MODEL), lambda i: (0, 0)),
            pl.BlockSpec((POOL_WIDTH, D_MODEL), lambda i: (1, 0)),
            pl.BlockSpec((1, D_MODEL), lambda i: (0, 0)),
        ],
        out_specs=pl.BlockSpec((tm, D_MODEL), lambda i: (i, 0)),
        out_shape=jax.ShapeDtypeStruct((t, D_MODEL), F32),
        scratch_shapes=[pltpu.VMEM((tm + POOL_HALO, POOL_WIDTH), F32),
                        pltpu.VMEM((tm, POOL_WIDTH), BF16)],
        compiler_params=pltpu.CompilerParams(
            dimension_semantics=("arbitrary",),
            vmem_limit_bytes=VMEM_LIMIT_BYTES),
        name="mixout",
    )(x, a, u, u, pool_w, pool_scale, w_o, w_o, g)


def _gelu_tanh(x):
    return 0.5 * x * (1.0 + jnp.tanh(math.sqrt(2.0 / math.pi) * (x + 0.044715 * (x * x * x))))


def _ffn_kernel(tiles_per_seq, x_ref, g_ref, wg_ref, wv_ref, cwg_ref, cwv_ref, cbg_ref, cbv_ref,
                wd_ref, pg_ref, o_ref, h_ref, y_ref, ug_ref, uv_ref, carry_g_ref, carry_v_ref):
    tm = FFN_TM
    i = pl.program_id(0)
    j = pl.program_id(1)
    first_in_seq = (i % tiles_per_seq) == 0

    @pl.when(j == 0)
    def _():
        x = x_ref[...]
        h_ref[...] = ((x * _rms_scale(x)) * g_ref[...]).astype(BF16)

    def conv(w_ref, cw_ref, cb_ref, buf_ref, carry_ref):
        buf_ref[0:CONV_HALO, :] = jnp.where(first_in_seq, 0.0, carry_ref[j])
        buf_ref[CONV_HALO:, :] = jnp.dot(h_ref[...], w_ref[...], preferred_element_type=F32)
        carry_ref[j] = buf_ref[tm:tm + CONV_HALO, :]
        out = cb_ref[...]
        for tap in range(CONV_WIDTH):
            start = CONV_HALO - (CONV_WIDTH - 1) + tap
            out = out + buf_ref[start:start + tm, :] * cw_ref[tap:tap + 1, :]
        return out

    gate = conv(wg_ref, cwg_ref, cbg_ref, ug_ref, carry_g_ref)
    val = conv(wv_ref, cwv_ref, cbv_ref, uv_ref, carry_v_ref)
    act = (_gelu_tanh(gate) * val).astype(BF16)
    y = jnp.dot(act, wd_ref[...], preferred_element_type=F32)

    @pl.when(j == 0)
    def _():
        y_ref[...] = y

    @pl.when(j > 0)
    def _():
        y_ref[...] += y

    @pl.when(j == pl.num_programs(1) - 1)
    def _():
        yy = y_ref[...]
        o_ref[...] = x_ref[...] + (yy * _rms_scale(yy)) * pg_ref[...]


def _ffn(x, g, w_up, conv_w, conv_b, w_down, post_g, seq_len):
    t = x.shape[0]
    tm, tf = FFN_TM, FFN_TF
    n_f = D_FF // tf
    tiles_per_seq = seq_len // tm
    return pl.pallas_call(
        functools.partial(_ffn_kernel, tiles_per_seq),
        grid=(t // tm, n_f),
        in_specs=[
            pl.BlockSpec((tm, D_MODEL), lambda i, j: (i, 0)),
            pl.BlockSpec((1, D_MODEL), lambda i, j: (0, 0)),
            pl.BlockSpec((D_MODEL, tf), lambda i, j: (0, j)),
            pl.BlockSpec((D_MODEL, tf), lambda i, j: (0, j + n_f)),
            pl.BlockSpec((CONV_WIDTH, tf), lambda i, j: (0, j)),
            pl.BlockSpec((CONV_WIDTH, tf), lambda i, j: (0, j + n_f)),
            pl.BlockSpec((1, tf), lambda i, j: (0, j)),
            pl.BlockSpec((1, tf), lambda i, j: (0, j + n_f)),
            pl.BlockSpec((tf, D_MODEL), lambda i, j: (j, 0)),
            pl.BlockSpec((1, D_MODEL), lambda i, j: (0, 0)),
        ],
        out_specs=pl.BlockSpec((tm, D_MODEL), lambda i, j: (i, 0)),
        out_shape=jax.ShapeDtypeStruct((t, D_MODEL), F32),
        scratch_shapes=[
            pltpu.VMEM((tm, D_MODEL), BF16),
            pltpu.VMEM((tm, D_MODEL), F32),
            pltpu.VMEM((tm + CONV_HALO, tf), F32),
            pltpu.VMEM((tm + CONV_HALO, tf), F32),
            pltpu.VMEM((n_f, CONV_HALO, tf), F32),
            pltpu.VMEM((n_f, CONV_HALO, tf), F32),
        ],
        compiler_params=pltpu.CompilerParams(
            dimension_semantics=("arbitrary", "arbitrary"),
            vmem_limit_bytes=VMEM_LIMIT_BYTES),
        name="ffn",
    )(x, g, w_up, w_up, conv_w, conv_w, conv_b, conv_b, w_down, post_g)


def kernel(x, pre_mix_g, w_in, rel_bias, pool_w, pool_scale, w_o, post_mix_g,
           pre_ffn_g, w_up, conv_w, conv_b, w_down, post_ffn_g):
    b, s, d = x.shape
    depth = w_in.shape[0]
    xt = x.reshape(b * s, d)
    for l in range(depth):
        qkv, u = _inproj(xt, pre_mix_g[l][None], w_in[l].astype(BF16))
        a = _attention(qkv, _band_bias(rel_bias[l]), s)
        xt = _mixout(xt, a, u, pool_w[l].astype(BF16), pool_scale[l][None], w_o[l].astype(BF16),
                     post_mix_g[l][None], s)
        xt = _ffn(xt, pre_ffn_g[l][None], w_up[l].astype(BF16), conv_w[l], conv_b[l][None],
                  w_down[l].astype(BF16), post_ffn_g[l][None], s)
    return xt.reshape(b, s, d)
```

```python
import functools
import math

import jax
import jax.numpy as jnp
from jax import lax
from jax.experimental import pallas as pl
from jax.experimental.pallas import tpu as pltpu

F32 = jnp.float32
BF16 = jnp.bfloat16

D_MODEL = 2048
CHUNK = 64
LEFT_CHUNKS = 8
ATTN_WIDTH = 1024
POOL_WIDTH = 1024
HEAD_DIM = 128
N_HEADS = ATTN_WIDTH // HEAD_DIM
REL_CLIP = 128
POOL_WINDOWS = (2, 4, 8, 16)
POOL_GROUP = POOL_WIDTH // len(POOL_WINDOWS)
IN_WIDTH = 3 * ATTN_WIDTH + POOL_WIDTH
D_FF = 5632
CONV_WIDTH = 3
NORM_EPS = 1e-6
MASK_VALUE = -1e30
LOG2_E = math.log2(math.e)
QK_SCALE = LOG2_E / math.sqrt(HEAD_DIM)
KV_COL_BLOCK = 0
Q_COL_BLOCK = 2
U_COL_BLOCK = 3

V7X_VMEM_BYTES = 64 * 1024 * 1024
VMEM_LIMIT_BYTES = V7X_VMEM_BYTES - 6 * 1024 * 1024
V7X_MXU_COLS = 256

IN_TM = 512
IN_TN = 1024
ATT_BQ = 2 * CHUNK
ATT_NKB = LEFT_CHUNKS * CHUNK // ATT_BQ + 1
ATT_BK = ATT_NKB * ATT_BQ
MIX_TM = 512
MIX_ROW_SPLIT = 2
POOL_HALO = 16
FFN_TM = 512
FFN_TF = 512
FFN_ROW_SPLIT = 2
CONV_HALO = 8


def _rms_scale(x):
    return lax.rsqrt(jnp.mean(x * x, axis=-1, keepdims=True) + NORM_EPS)


def _row(vec):
    return vec.reshape(vec.shape[0], 1, vec.shape[1])


def _inproj_kernel(x_ref, g_ref, w_ref, z_ref, u_ref, h_ref):
    j = pl.program_id(1)

    @pl.when(j == 0)
    def _():
        x = x_ref[...]
        h_ref[...] = ((x * _rms_scale(x)) * g_ref[...]).astype(BF16)

    z = jnp.dot(h_ref[...], w_ref[...], preferred_element_type=F32)
    scale = jnp.where(j == Q_COL_BLOCK, QK_SCALE, 1.0).astype(F32)
    z_ref[...] = (z * scale).astype(BF16)
    u_ref[...] = z


def _inproj(layer, x, g, w):
    t = x.shape[0]
    grid = (t // IN_TM, IN_WIDTH // IN_TN)
    return pl.pallas_call(
        _inproj_kernel,
        grid=grid,
        in_specs=[
            pl.BlockSpec((IN_TM, D_MODEL), lambda i, j: (i, 0)),
            pl.BlockSpec((None, 1, D_MODEL), lambda i, j: (layer, 0, 0)),
            pl.BlockSpec((None, D_MODEL, IN_TN), lambda i, j: (layer, 0, j)),
        ],
        out_specs=[
            pl.BlockSpec((IN_TM, IN_TN), lambda i, j: (i, j)),
            pl.BlockSpec((IN_TM, POOL_WIDTH), lambda i, j: (i, 0)),
        ],
        out_shape=[
            jax.ShapeDtypeStruct((t, IN_WIDTH), BF16),
            jax.ShapeDtypeStruct((t, POOL_WIDTH), F32),
        ],
        scratch_shapes=[pltpu.VMEM((IN_TM, D_MODEL), BF16)],
        compiler_params=pltpu.CompilerParams(
            dimension_semantics=("arbitrary", "arbitrary"),
            vmem_limit_bytes=VMEM_LIMIT_BYTES),
        name="inproj",
    )(x, g, w)


def _band_bias(table):
    n_heads = table.shape[0]
    band_keys = (LEFT_CHUNKS + 1) * CHUNK
    far = band_keys - REL_CLIP
    near = REL_CLIP + CHUNK - 1
    table = table.astype(F32)
    ext = jnp.concatenate(
        [jnp.broadcast_to(table[:, 2 * REL_CLIP:], (n_heads, far)),
         table[:, 2 * REL_CLIP - near:2 * REL_CLIP][:, ::-1]], axis=1)
    rows = jnp.stack([ext[:, CHUNK - 1 - qi:CHUNK - 1 - qi + band_keys] for qi in range(CHUNK)], axis=1)
    rows = rows * LOG2_E
    pad = ATT_BK - band_keys
    masked = jnp.full((n_heads, CHUNK, pad), MASK_VALUE, F32)
    blocks = [jnp.concatenate([masked[:, :, :c * CHUNK], rows, masked[:, :, :pad - c * CHUNK]], axis=2)
              for c in range(ATT_BQ // CHUNK)]
    bias = jnp.concatenate(blocks, axis=1)
    key_block = lax.broadcasted_iota(jnp.int32, (ATT_NKB, 1, 1, ATT_BK), 3) // ATT_BQ
    n_masked = lax.broadcasted_iota(jnp.int32, (ATT_NKB, 1, 1, ATT_BK), 0)
    return jnp.where(key_block < n_masked, F32(MASK_VALUE), bias[None])


def _attn_kernel(q_ref, *refs):
    kv_refs = refs[:ATT_NKB]
    bias_ref, o_ref, kv_ref = refs[ATT_NKB:]

    for d in range(ATT_NKB):
        kv_ref[d * ATT_BQ:(d + 1) * ATT_BQ, :] = kv_refs[d][...]

    def scores(h):
        hs = slice(h * HEAD_DIM, (h + 1) * HEAD_DIM)
        s = lax.dot_general(q_ref[:, hs], kv_ref[:, hs], (((1,), (1,)), ((), ())),
                            preferred_element_type=F32)
        return s + bias_ref[h]

    def finish(h, s):
        vs = slice(ATTN_WIDTH + h * HEAD_DIM, ATTN_WIDTH + (h + 1) * HEAD_DIM)
        m = jnp.max(s, axis=-1, keepdims=True)
        p = jnp.exp2(s - m)
        l = jnp.sum(p, axis=-1, keepdims=True)
        o = jnp.dot(p.astype(BF16), kv_ref[:, vs], preferred_element_type=F32)
        o_ref[:, h * HEAD_DIM:(h + 1) * HEAD_DIM] = (o / l).astype(BF16)

    s_next = scores(0)
    for h in range(N_HEADS):
        s = s_next
        if h + 1 < N_HEADS:
            s_next = scores(h + 1)
        finish(h, s)


def _attention(z, bias, seq_len):
    t = z.shape[0]
    blocks_per_seq = seq_len // ATT_BQ

    def kv_map(d):
        def index_map(i):
            first = (i // blocks_per_seq) * blocks_per_seq
            return (jnp.maximum(i - (ATT_NKB - 1) + d, first), KV_COL_BLOCK)
        return index_map

    def bias_map(i):
        return (jnp.maximum(ATT_NKB - 1 - i % blocks_per_seq, 0), 0, 0, 0)

    in_specs = [pl.BlockSpec((ATT_BQ, ATTN_WIDTH), lambda i: (i, Q_COL_BLOCK))]
    in_specs += [pl.BlockSpec((ATT_BQ, 2 * ATTN_WIDTH), kv_map(d)) for d in range(ATT_NKB)]
    in_specs += [pl.BlockSpec((None, N_HEADS, ATT_BQ, ATT_BK), bias_map)]
    return pl.pallas_call(
        _attn_kernel,
        grid=(t // ATT_BQ,),
        in_specs=in_specs,
        out_specs=pl.BlockSpec((ATT_BQ, ATTN_WIDTH), lambda i: (i, 0)),
        out_shape=jax.ShapeDtypeStruct((t, ATTN_WIDTH), BF16),
        scratch_shapes=[pltpu.VMEM((ATT_BK, 2 * ATTN_WIDTH), BF16)],
        compiler_params=pltpu.CompilerParams(
            dimension_semantics=("arbitrary",),
            vmem_limit_bytes=VMEM_LIMIT_BYTES),
        name="attention",
    )(z, *([z] * ATT_NKB), bias)


def _mixout_kernel(tiles_per_seq, x_ref, a_ref, u_ref, uh_ref, pw_ref, ps_ref, woa_ref, wop_ref,
                   g_ref, o_ref, ebuf_ref, p_ref, m_ref):
    tm = MIX_TM
    sub = tm // MIX_ROW_SPLIT
    i_in_seq = pl.program_id(0) % tiles_per_seq

    ebuf_ref[0:POOL_HALO, :] = jnp.where(i_in_seq == 0, 0.0, uh_ref[...])
    ebuf_ref[POOL_HALO:, :] = u_ref[...]

    n_groups = len(POOL_WINDOWS)
    a_sub = tm // n_groups
    head_pos = i_in_seq * tm + lax.broadcasted_iota(jnp.int32, (POOL_HALO, POOL_GROUP), 0)
    for g, w in enumerate(POOL_WINDOWS):
        rows = slice(g * a_sub, (g + 1) * a_sub)
        m_ref[rows, :] = jnp.dot(a_ref[rows, :], woa_ref[...], preferred_element_type=F32)

        cs = slice(g * POOL_GROUP, (g + 1) * POOL_GROUP)
        tok = ebuf_ref[POOL_HALO:, cs]
        wsum = tok
        for k in range(1, w):
            wsum = wsum + ebuf_ref[POOL_HALO - k:POOL_HALO - k + tm, cs]
        body = wsum * (1.0 / w) - tok
        head_count = jnp.minimum(head_pos + 1, w).astype(F32)
        head = wsum[:POOL_HALO] / head_count - tok[:POOL_HALO]
        pooled = jnp.concatenate([head, body[POOL_HALO:]], axis=0).astype(BF16)
        mixed = jnp.dot(pooled, pw_ref[g], preferred_element_type=F32)
        p_ref[:, cs] = (mixed * ps_ref[:, cs]).astype(BF16)

    for r in range(MIX_ROW_SPLIT):
        rows = slice(r * sub, (r + 1) * sub)
        m = m_ref[rows, :] + jnp.dot(p_ref[rows, :], wop_ref[...], preferred_element_type=F32)
        o_ref[rows, :] = x_ref[rows, :] + (m * _rms_scale(m)) * g_ref[...]


def _mixout(layer, x, a, u, pool_w, pool_scale, w_o, g, seq_len):
    t = x.shape[0]
    tm = MIX_TM
    tiles_per_seq = seq_len // tm
    halo_blocks = tm // POOL_HALO
    n_groups = len(POOL_WINDOWS)
    return pl.pallas_call(
        functools.partial(_mixout_kernel, tiles_per_seq),
        grid=(t // tm,),
        in_specs=[
            pl.BlockSpec((tm, D_MODEL), lambda i: (i, 0)),
            pl.BlockSpec((tm, ATTN_WIDTH), lambda i: (i, 0)),
            pl.BlockSpec((tm, POOL_WIDTH), lambda i: (i, 0)),
            pl.BlockSpec((POOL_HALO, POOL_WIDTH), lambda i: (jnp.maximum(i * halo_blocks - 1, 0), 0)),
            pl.BlockSpec((None, n_groups, POOL_GROUP, POOL_GROUP), lambda i: (layer, 0, 0, 0)),
            pl.BlockSpec((None, 1, POOL_WIDTH), lambda i: (layer, 0, 0)),
            pl.BlockSpec((None, ATTN_WIDTH, D_MODEL), lambda i: (layer, 0, 0)),
            pl.BlockSpec((None, POOL_WIDTH, D_MODEL), lambda i: (layer, 1, 0)),
            pl.BlockSpec((None, 1, D_MODEL), lambda i: (layer, 0, 0)),
        ],
        out_specs=pl.BlockSpec((tm, D_MODEL), lambda i: (i, 0)),
        out_shape=jax.ShapeDtypeStruct((t, D_MODEL), F32),
        scratch_shapes=[pltpu.VMEM((tm + POOL_HALO, POOL_WIDTH), F32),
                        pltpu.VMEM((tm, POOL_WIDTH), BF16),
                        pltpu.VMEM((tm, D_MODEL), F32)],
        compiler_params=pltpu.CompilerParams(
            dimension_semantics=("arbitrary",),
            vmem_limit_bytes=VMEM_LIMIT_BYTES),
        name="mixout",
    )(x, a, u, u, pool_w, pool_scale, w_o, w_o, g)


def _gelu_tanh(x):
    return 0.5 * x * (1.0 + jnp.tanh(math.sqrt(2.0 / math.pi) * (x + 0.044715 * (x * x * x))))


def _ffn_kernel(tiles_per_seq, x_ref, g_ref, wg_ref, wv_ref, cwg_ref, cwv_ref, cbg_ref, cbv_ref,
                wd_ref, pg_ref, o_ref, h_ref, y_ref, ug_ref, uv_ref, carry_g_ref, carry_v_ref):
    tm = FFN_TM
    i = pl.program_id(0)
    j = pl.program_id(1)
    first_in_seq = (i % tiles_per_seq) == 0

    @pl.when(j == 0)
    def _():
        x = x_ref[...]
        h_ref[...] = ((x * _rms_scale(x)) * g_ref[...]).astype(BF16)
        y_ref[...] = jnp.zeros_like(y_ref)

    sub = tm // FFN_ROW_SPLIT

    def up(s, w_ref, buf_ref):
        rows = slice(s * sub, (s + 1) * sub)
        buf_ref[CONV_HALO + s * sub:CONV_HALO + (s + 1) * sub, :] = jnp.dot(
            h_ref[rows, :], w_ref[...], preferred_element_type=F32)

    def conv(s, cw_ref, cb_ref, buf_ref):
        out = cb_ref[...]
        for tap in range(CONV_WIDTH):
            start = CONV_HALO - (CONV_WIDTH - 1) + tap + s * sub
            out = out + buf_ref[start:start + sub, :] * cw_ref[tap:tap + 1, :]
        return out

    ug_ref[0:CONV_HALO, :] = jnp.where(first_in_seq, 0.0, carry_g_ref[j])
    uv_ref[0:CONV_HALO, :] = jnp.where(first_in_seq, 0.0, carry_v_ref[j])
    for s in range(FFN_ROW_SPLIT):
        up(s, wg_ref, ug_ref)
        up(s, wv_ref, uv_ref)
    carry_g_ref[j] = ug_ref[tm:tm + CONV_HALO, :]
    carry_v_ref[j] = uv_ref[tm:tm + CONV_HALO, :]
    for s in range(FFN_ROW_SPLIT):
        rows = slice(s * sub, (s + 1) * sub)
        gate = conv(s, cwg_ref, cbg_ref, ug_ref)
        val = conv(s, cwv_ref, cbv_ref, uv_ref)
        act = (_gelu_tanh(gate) * val).astype(BF16)
        y_ref[rows, :] += jnp.dot(act, wd_ref[...], preferred_element_type=F32)

    @pl.when(j == pl.num_programs(1) - 1)
    def _():
        yy = y_ref[...]
        o_ref[...] = x_ref[...] + (yy * _rms_scale(yy)) * pg_ref[...]


def _ffn(layer, x, g, w_up, conv_w, conv_b, w_down, post_g, seq_len):
    t = x.shape[0]
    tm, tf = FFN_TM, FFN_TF
    n_f = D_FF // tf
    tiles_per_seq = seq_len // tm
    return pl.pallas_call(
        functools.partial(_ffn_kernel, tiles_per_seq),
        grid=(t // tm, n_f),
        in_specs=[
            pl.BlockSpec((tm, D_MODEL), lambda i, j: (i, 0)),
            pl.BlockSpec((None, 1, D_MODEL), lambda i, j: (layer, 0, 0)),
            pl.BlockSpec((None, D_MODEL, tf), lambda i, j: (layer, 0, j)),
            pl.BlockSpec((None, D_MODEL, tf), lambda i, j: (layer, 0, j + n_f)),
            pl.BlockSpec((None, CONV_WIDTH, tf), lambda i, j: (layer, 0, j)),
            pl.BlockSpec((None, CONV_WIDTH, tf), lambda i, j: (layer, 0, j + n_f)),
            pl.BlockSpec((None, 1, tf), lambda i, j: (layer, 0, j)),
            pl.BlockSpec((None, 1, tf), lambda i, j: (layer, 0, j + n_f)),
            pl.BlockSpec((None, tf, D_MODEL), lambda i, j: (layer, j, 0)),
            pl.BlockSpec((None, 1, D_MODEL), lambda i, j: (layer, 0, 0)),
        ],
        out_specs=pl.BlockSpec((tm, D_MODEL), lambda i, j: (i, 0)),
        out_shape=jax.ShapeDtypeStruct((t, D_MODEL), F32),
        scratch_shapes=[
            pltpu.VMEM((tm, D_MODEL), BF16),
            pltpu.VMEM((tm, D_MODEL), F32),
            pltpu.VMEM((tm + CONV_HALO, tf), F32),
            pltpu.VMEM((tm + CONV_HALO, tf), F32),
            pltpu.VMEM((n_f, CONV_HALO, tf), F32),
            pltpu.VMEM((n_f, CONV_HALO, tf), F32),
        ],
        compiler_params=pltpu.CompilerParams(
            dimension_semantics=("arbitrary", "arbitrary"),
            vmem_limit_bytes=VMEM_LIMIT_BYTES),
        name="ffn",
    )(x, g, w_up, w_up, conv_w, conv_w, conv_b, conv_b, w_down, post_g)


def kernel(x, pre_mix_g, w_in, rel_bias, pool_w, pool_scale, w_o, post_mix_g,
           pre_ffn_g, w_up, conv_w, conv_b, w_down, post_ffn_g):
    b, s, d = x.shape
    depth = w_in.shape[0]
    w_in = jnp.concatenate([w_in[..., ATTN_WIDTH:3 * ATTN_WIDTH], w_in[..., :ATTN_WIDTH],
                            w_in[..., 3 * ATTN_WIDTH:]], axis=-1)
    w_in, pool_w, w_o, w_up, w_down = (w.astype(BF16) for w in (w_in, pool_w, w_o, w_up, w_down))
    pre_mix_g, pool_scale, post_mix_g, pre_ffn_g, conv_b, post_ffn_g = (
        _row(v) for v in (pre_mix_g, pool_scale, post_mix_g, pre_ffn_g, conv_b, post_ffn_g))
    xt = x.reshape(b * s, d)
    for l in range(depth):
        z, u = _inproj(l, xt, pre_mix_g, w_in)
        a = _attention(z, _band_bias(rel_bias[l]), s)
        xt = _mixout(l, xt, a, u, pool_w, pool_scale, w_o, post_mix_g, s)
        xt = _ffn(l, xt, pre_ffn_g, w_up, conv_w, conv_b, w_down, post_ffn_g, s)
    return xt.reshape(b, s, d)
```

```python
import functools
import math

import jax
import jax.numpy as jnp
from jax import lax
from jax.experimental import pallas as pl
from jax.experimental.pallas import tpu as pltpu

F32 = jnp.float32
BF16 = jnp.bfloat16

D_MODEL = 2048
CHUNK = 64
LEFT_CHUNKS = 8
ATTN_WIDTH = 1024
POOL_WIDTH = 1024
HEAD_DIM = 128
N_HEADS = ATTN_WIDTH // HEAD_DIM
REL_CLIP = 128
POOL_WINDOWS = (2, 4, 8, 16)
POOL_GROUP = POOL_WIDTH // len(POOL_WINDOWS)
IN_WIDTH = 3 * ATTN_WIDTH + POOL_WIDTH
D_FF = 5632
CONV_WIDTH = 3
NORM_EPS = 1e-6
MASK_VALUE = -1e30
LOG2_E = math.log2(math.e)
QK_SCALE = LOG2_E / math.sqrt(HEAD_DIM)
KV_COL_BLOCK = 0
Q_COL_BLOCK = 2
U_COL_BLOCK = 3

V7X_VMEM_BYTES = 64 * 1024 * 1024
VMEM_LIMIT_BYTES = V7X_VMEM_BYTES - 6 * 1024 * 1024
V7X_MXU_COLS = 256

IN_TM = 1024
IN_TN = 1024
ATT_BQ = 2 * CHUNK
ATT_NKB = LEFT_CHUNKS * CHUNK // ATT_BQ + 1
ATT_BK = ATT_NKB * ATT_BQ
MIX_TM = 512
MIX_ROW_SPLIT = 2
POOL_HALO = 16
FFN_TM = 1024
FFN_TF = 512
FFN_ROW_SPLIT = 4
CONV_HALO = 8


def _rms_scale(x):
    return lax.rsqrt(jnp.mean(x * x, axis=-1, keepdims=True) + NORM_EPS)


def _row(vec):
    return vec.reshape(vec.shape[0], 1, vec.shape[1])


def _inproj_kernel(x_ref, g_ref, w_ref, z_ref, u_ref, h_ref):
    j = pl.program_id(1)

    @pl.when(j == 0)
    def _():
        x = x_ref[...]
        h_ref[...] = ((x * _rms_scale(x)) * g_ref[...]).astype(BF16)

    z = jnp.dot(h_ref[...], w_ref[...], preferred_element_type=F32)
    scale = jnp.where(j == Q_COL_BLOCK, QK_SCALE, 1.0).astype(F32)
    z_ref[...] = (z * scale).astype(BF16)
    u_ref[...] = z


def _inproj(layer, x, g, w):
    t = x.shape[0]
    grid = (t // IN_TM, IN_WIDTH // IN_TN)
    return pl.pallas_call(
        _inproj_kernel,
        grid=grid,
        in_specs=[
            pl.BlockSpec((IN_TM, D_MODEL), lambda i, j: (i, 0)),
            pl.BlockSpec((None, 1, D_MODEL), lambda i, j: (layer, 0, 0)),
            pl.BlockSpec((None, D_MODEL, IN_TN), lambda i, j: (layer, 0, j)),
        ],
        out_specs=[
            pl.BlockSpec((IN_TM, IN_TN), lambda i, j: (i, j)),
            pl.BlockSpec((IN_TM, POOL_WIDTH), lambda i, j: (i, 0)),
        ],
        out_shape=[
            jax.ShapeDtypeStruct((t, IN_WIDTH), BF16),
            jax.ShapeDtypeStruct((t, POOL_WIDTH), F32),
        ],
        scratch_shapes=[pltpu.VMEM((IN_TM, D_MODEL), BF16)],
        compiler_params=pltpu.CompilerParams(
            dimension_semantics=("arbitrary", "arbitrary"),
            vmem_limit_bytes=VMEM_LIMIT_BYTES),
        name="inproj",
    )(x, g, w)


def _band_bias(table):
    n_heads = table.shape[0]
    band_keys = (LEFT_CHUNKS + 1) * CHUNK
    far = band_keys - REL_CLIP
    near = REL_CLIP + CHUNK - 1
    table = table.astype(F32)
    ext = jnp.concatenate(
        [jnp.broadcast_to(table[:, 2 * REL_CLIP:], (n_heads, far)),
         table[:, 2 * REL_CLIP - near:2 * REL_CLIP][:, ::-1]], axis=1)
    rows = jnp.stack([ext[:, CHUNK - 1 - qi:CHUNK - 1 - qi + band_keys] for qi in range(CHUNK)], axis=1)
    rows = rows * LOG2_E
    pad = ATT_BK - band_keys
    masked = jnp.full((n_heads, CHUNK, pad), MASK_VALUE, F32)
    blocks = [jnp.concatenate([masked[:, :, :c * CHUNK], rows, masked[:, :, :pad - c * CHUNK]], axis=2)
              for c in range(ATT_BQ // CHUNK)]
    bias = jnp.concatenate(blocks, axis=1)
    key_block = lax.broadcasted_iota(jnp.int32, (ATT_NKB, 1, 1, ATT_BK), 3) // ATT_BQ
    n_masked = lax.broadcasted_iota(jnp.int32, (ATT_NKB, 1, 1, ATT_BK), 0)
    return jnp.where(key_block < n_masked, F32(MASK_VALUE), bias[None])


def _attn_kernel(kvq_ref, bias_ref, o_ref, kv_ref):
    @pl.when(pl.program_id(0) == 0)
    def _():
        kv_ref[...] = jnp.zeros_like(kv_ref)

    keep = ATT_BK - ATT_BQ
    kv_ref[0:keep, :] = kv_ref[ATT_BQ:, :]
    kv_ref[keep:, :] = kvq_ref[:, 0:2 * ATTN_WIDTH]

    def scores(h):
        hs = slice(h * HEAD_DIM, (h + 1) * HEAD_DIM)
        qs = slice(2 * ATTN_WIDTH + h * HEAD_DIM, 2 * ATTN_WIDTH + (h + 1) * HEAD_DIM)
        s = lax.dot_general(kvq_ref[:, qs], kv_ref[:, hs], (((1,), (1,)), ((), ())),
                            preferred_element_type=F32)
        return s + bias_ref[h]

    def finish(h, s):
        vs = slice(ATTN_WIDTH + h * HEAD_DIM, ATTN_WIDTH + (h + 1) * HEAD_DIM)
        m = jnp.max(s, axis=-1, keepdims=True)
        p = jnp.exp2(s - m)
        l = jnp.sum(p, axis=-1, keepdims=True)
        o = jnp.dot(p.astype(BF16), kv_ref[:, vs], preferred_element_type=F32)
        o_ref[:, h * HEAD_DIM:(h + 1) * HEAD_DIM] = (o / l).astype(BF16)

    s_next = scores(0)
    for h in range(N_HEADS):
        s = s_next
        if h + 1 < N_HEADS:
            s_next = scores(h + 1)
        finish(h, s)


def _attention(z, bias, seq_len):
    t = z.shape[0]
    blocks_per_seq = seq_len // ATT_BQ

    def bias_map(i):
        return (jnp.maximum(ATT_NKB - 1 - i % blocks_per_seq, 0), 0, 0, 0)

    return pl.pallas_call(
        _attn_kernel,
        grid=(t // ATT_BQ,),
        in_specs=[pl.BlockSpec((ATT_BQ, 3 * ATTN_WIDTH), lambda i: (i, 0)),
                  pl.BlockSpec((None, N_HEADS, ATT_BQ, ATT_BK), bias_map)],
        out_specs=pl.BlockSpec((ATT_BQ, ATTN_WIDTH), lambda i: (i, 0)),
        out_shape=jax.ShapeDtypeStruct((t, ATTN_WIDTH), BF16),
        scratch_shapes=[pltpu.VMEM((ATT_BK, 2 * ATTN_WIDTH), BF16)],
        compiler_params=pltpu.CompilerParams(
            dimension_semantics=("arbitrary",),
            vmem_limit_bytes=VMEM_LIMIT_BYTES),
        name="attention",
    )(z, bias)


def _mixout_kernel(tiles_per_seq, x_ref, a_ref, u_ref, uh_ref, pw_ref, ps_ref, woa_ref, wop_ref,
                   g_ref, o_ref, ebuf_ref, p_ref, m_ref):
    tm = MIX_TM
    sub = tm // MIX_ROW_SPLIT
    i_in_seq = pl.program_id(0) % tiles_per_seq

    ebuf_ref[0:POOL_HALO, :] = jnp.where(i_in_seq == 0, 0.0, uh_ref[...])
    ebuf_ref[POOL_HALO:, :] = u_ref[...]

    n_groups = len(POOL_WINDOWS)
    a_sub = tm // n_groups
    head_pos = i_in_seq * tm + lax.broadcasted_iota(jnp.int32, (POOL_HALO, POOL_GROUP), 0)
    for g, w in enumerate(POOL_WINDOWS):
        rows = slice(g * a_sub, (g + 1) * a_sub)
        m_ref[rows, :] = jnp.dot(a_ref[rows, :], woa_ref[...], preferred_element_type=F32)

        cs = slice(g * POOL_GROUP, (g + 1) * POOL_GROUP)
        tok = ebuf_ref[POOL_HALO:, cs]
        wsum = tok
        for k in range(1, w):
            wsum = wsum + ebuf_ref[POOL_HALO - k:POOL_HALO - k + tm, cs]
        body = wsum * (1.0 / w) - tok
        head_count = jnp.minimum(head_pos + 1, w).astype(F32)
        head = wsum[:POOL_HALO] / head_count - tok[:POOL_HALO]
        pooled = jnp.concatenate([head, body[POOL_HALO:]], axis=0).astype(BF16)
        mixed = jnp.dot(pooled, pw_ref[g], preferred_element_type=F32)
        p_ref[:, cs] = (mixed * ps_ref[:, cs]).astype(BF16)

    for r in range(MIX_ROW_SPLIT):
        rows = slice(r * sub, (r + 1) * sub)
        m = m_ref[rows, :] + jnp.dot(p_ref[rows, :], wop_ref[...], preferred_element_type=F32)
        o_ref[rows, :] = x_ref[rows, :] + (m * _rms_scale(m)) * g_ref[...]


def _mixout(layer, x, a, u, pool_w, pool_scale, w_o, g, seq_len):
    t = x.shape[0]
    tm = MIX_TM
    tiles_per_seq = seq_len // tm
    halo_blocks = tm // POOL_HALO
    n_groups = len(POOL_WINDOWS)
    return pl.pallas_call(
        functools.partial(_mixout_kernel, tiles_per_seq),
        grid=(t // tm,),
        in_specs=[
            pl.BlockSpec((tm, D_MODEL), lambda i: (i, 0)),
            pl.BlockSpec((tm, ATTN_WIDTH), lambda i: (i, 0)),
            pl.BlockSpec((tm, POOL_WIDTH), lambda i: (i, 0)),
            pl.BlockSpec((POOL_HALO, POOL_WIDTH), lambda i: (jnp.maximum(i * halo_blocks - 1, 0), 0)),
            pl.BlockSpec((None, n_groups, POOL_GROUP, POOL_GROUP), lambda i: (layer, 0, 0, 0)),
            pl.BlockSpec((None, 1, POOL_WIDTH), lambda i: (layer, 0, 0)),
            pl.BlockSpec((None, ATTN_WIDTH, D_MODEL), lambda i: (layer, 0, 0)),
            pl.BlockSpec((None, POOL_WIDTH, D_MODEL), lambda i: (layer, 1, 0)),
            pl.BlockSpec((None, 1, D_MODEL), lambda i: (layer, 0, 0)),
        ],
        out_specs=pl.BlockSpec((tm, D_MODEL), lambda i: (i, 0)),
        out_shape=jax.ShapeDtypeStruct((t, D_MODEL), F32),
        scratch_shapes=[pltpu.VMEM((tm + POOL_HALO, POOL_WIDTH), F32),
                        pltpu.VMEM((tm, POOL_WIDTH), BF16),
                        pltpu.VMEM((tm, D_MODEL), F32)],
        compiler_params=pltpu.CompilerParams(
            dimension_semantics=("arbitrary",),
            vmem_limit_bytes=VMEM_LIMIT_BYTES),
        name="mixout",
    )(x, a, u, u, pool_w, pool_scale, w_o, w_o, g)


def _gelu_tanh(x):
    return 0.5 * x * (1.0 + jnp.tanh(math.sqrt(2.0 / math.pi) * (x + 0.044715 * (x * x * x))))


def _ffn_kernel(tiles_per_seq, x_ref, g_ref, wg_ref, wv_ref, cwg_ref, cwv_ref, cbg_ref, cbv_ref,
                wd_ref, pg_ref, o_ref, h_ref, ug_ref, uv_ref, carry_g_ref, carry_v_ref):
    tm = FFN_TM
    i = pl.program_id(0)
    j = pl.program_id(1)
    first_in_seq = (i % tiles_per_seq) == 0
    y_ref = o_ref

    @pl.when(j == 0)
    def _():
        x = x_ref[...]
        h_ref[...] = ((x * _rms_scale(x)) * g_ref[...]).astype(BF16)
        y_ref[...] = jnp.zeros_like(y_ref)

    sub = tm // FFN_ROW_SPLIT

    def up(s, w_ref, buf_ref):
        rows = slice(s * sub, (s + 1) * sub)
        buf_ref[CONV_HALO + s * sub:CONV_HALO + (s + 1) * sub, :] = jnp.dot(
            h_ref[rows, :], w_ref[...], preferred_element_type=F32)

    def conv(s, cw_ref, cb_ref, buf_ref):
        out = cb_ref[...]
        for tap in range(CONV_WIDTH):
            start = CONV_HALO - (CONV_WIDTH - 1) + tap + s * sub
            out = out + buf_ref[start:start + sub, :] * cw_ref[tap:tap + 1, :]
        return out

    ug_ref[0:CONV_HALO, :] = jnp.where(first_in_seq, 0.0, carry_g_ref[j])
    uv_ref[0:CONV_HALO, :] = jnp.where(first_in_seq, 0.0, carry_v_ref[j])
    for s in range(FFN_ROW_SPLIT):
        up(s, wg_ref, ug_ref)
        up(s, wv_ref, uv_ref)
    carry_g_ref[j] = ug_ref[tm:tm + CONV_HALO, :]
    carry_v_ref[j] = uv_ref[tm:tm + CONV_HALO, :]
    for s in range(FFN_ROW_SPLIT):
        rows = slice(s * sub, (s + 1) * sub)
        gate = conv(s, cwg_ref, cbg_ref, ug_ref)
        val = conv(s, cwv_ref, cbv_ref, uv_ref)
        act = (_gelu_tanh(gate) * val).astype(BF16)
        y_ref[rows, :] += jnp.dot(act, wd_ref[...], preferred_element_type=F32)

    @pl.when(j == pl.num_programs(1) - 1)
    def _():
        yy = y_ref[...]
        o_ref[...] = x_ref[...] + (yy * _rms_scale(yy)) * pg_ref[...]


def _ffn(layer, x, g, w_up, conv_w, conv_b, w_down, post_g, seq_len):
    t = x.shape[0]
    tm, tf = FFN_TM, FFN_TF
    n_f = D_FF // tf
    tiles_per_seq = seq_len // tm
    return pl.pallas_call(
        functools.partial(_ffn_kernel, tiles_per_seq),
        grid=(t // tm, n_f),
        in_specs=[
            pl.BlockSpec((tm, D_MODEL), lambda i, j: (i, 0), pipeline_mode=pl.Buffered(1)),
            pl.BlockSpec((None, 1, D_MODEL), lambda i, j: (layer, 0, 0)),
            pl.BlockSpec((None, D_MODEL, tf), lambda i, j: (layer, 0, j)),
            pl.BlockSpec((None, D_MODEL, tf), lambda i, j: (layer, 0, j + n_f)),
            pl.BlockSpec((None, CONV_WIDTH, tf), lambda i, j: (layer, 0, j)),
            pl.BlockSpec((None, CONV_WIDTH, tf), lambda i, j: (layer, 0, j + n_f)),
            pl.BlockSpec((None, 1, tf), lambda i, j: (layer, 0, j)),
            pl.BlockSpec((None, 1, tf), lambda i, j: (layer, 0, j + n_f)),
            pl.BlockSpec((None, tf, D_MODEL), lambda i, j: (layer, j, 0)),
            pl.BlockSpec((None, 1, D_MODEL), lambda i, j: (layer, 0, 0)),
        ],
        out_specs=pl.BlockSpec((tm, D_MODEL), lambda i, j: (i, 0)),
        out_shape=jax.ShapeDtypeStruct((t, D_MODEL), F32),
        scratch_shapes=[
            pltpu.VMEM((tm, D_MODEL), BF16),
            pltpu.VMEM((tm + CONV_HALO, tf), F32),
            pltpu.VMEM((tm + CONV_HALO, tf), F32),
            pltpu.VMEM((n_f, CONV_HALO, tf), F32),
            pltpu.VMEM((n_f, CONV_HALO, tf), F32),
        ],
        compiler_params=pltpu.CompilerParams(
            dimension_semantics=("arbitrary", "arbitrary"),
            vmem_limit_bytes=VMEM_LIMIT_BYTES),
        name="ffn",
    )(x, g, w_up, w_up, conv_w, conv_w, conv_b, conv_b, w_down, post_g)


def kernel(x, pre_mix_g, w_in, rel_bias, pool_w, pool_scale, w_o, post_mix_g,
           pre_ffn_g, w_up, conv_w, conv_b, w_down, post_ffn_g):
    b, s, d = x.shape
    depth = w_in.shape[0]
    w_in = jnp.concatenate([w_in[..., ATTN_WIDTH:3 * ATTN_WIDTH], w_in[..., :ATTN_WIDTH],
                            w_in[..., 3 * ATTN_WIDTH:]], axis=-1)
    w_in, pool_w, w_o, w_up, w_down = (w.astype(BF16) for w in (w_in, pool_w, w_o, w_up, w_down))
    pre_mix_g, pool_scale, post_mix_g, pre_ffn_g, conv_b, post_ffn_g = (
        _row(v) for v in (pre_mix_g, pool_scale, post_mix_g, pre_ffn_g, conv_b, post_ffn_g))
    xt = x.reshape(b * s, d)
    for l in range(depth):
        z, u = _inproj(l, xt, pre_mix_g, w_in)
        a = _attention(z, _band_bias(rel_bias[l]), s)
        xt = _mixout(l, xt, a, u, pool_w, pool_scale, w_o, post_mix_g, s)
        xt = _ffn(l, xt, pre_ffn_g, w_up, conv_w, conv_b, w_down, post_ffn_g, s)
    return xt.reshape(b, s, d)
```

```python
import functools
import math

import jax
import jax.numpy as jnp
from jax import lax
from jax.experimental import pallas as pl
from jax.experimental.pallas import tpu as pltpu

F32 = jnp.float32
BF16 = jnp.bfloat16

D_MODEL = 2048
CHUNK = 64
LEFT_CHUNKS = 8
ATTN_WIDTH = 1024
POOL_WIDTH = 1024
HEAD_DIM = 128
N_HEADS = ATTN_WIDTH // HEAD_DIM
REL_CLIP = 128
POOL_WINDOWS = (2, 4, 8, 16)
POOL_GROUP = POOL_WIDTH // len(POOL_WINDOWS)
IN_WIDTH = 3 * ATTN_WIDTH + POOL_WIDTH
D_FF = 5632
CONV_WIDTH = 3
NORM_EPS = 1e-6
MASK_VALUE = -1e30
LOG2_E = math.log2(math.e)
QK_SCALE = LOG2_E / math.sqrt(HEAD_DIM)
KV_COL_BLOCK = 0
Q_COL_BLOCK = 2
U_COL_BLOCK = 3

V7X_VMEM_BYTES = 64 * 1024 * 1024
VMEM_LIMIT_BYTES = V7X_VMEM_BYTES - 6 * 1024 * 1024
V7X_MXU_COLS = 256

IN_TM = 1024
IN_TN = 1024
ATT_BQ = 2 * CHUNK
ATT_NKB = LEFT_CHUNKS * CHUNK // ATT_BQ + 1
ATT_BK = ATT_NKB * ATT_BQ
MIX_TM = 512
MIX_ROW_SPLIT = 2
POOL_HALO = 16
FFN_TM = 512
FFN_TF = 512
FFN_ROW_SPLIT = 2
CONV_HALO = 8


def _rms_scale(x):
    return lax.rsqrt(jnp.mean(x * x, axis=-1, keepdims=True) + NORM_EPS)


def _row(vec):
    return vec.reshape(vec.shape[0], 1, vec.shape[1])


def _inproj_kernel(x_ref, g_ref, w_ref, z_ref, u_ref, h_ref):
    j = pl.program_id(1)

    @pl.when(j == 0)
    def _():
        x = x_ref[...]
        h_ref[...] = ((x * _rms_scale(x)) * g_ref[...]).astype(BF16)

    z = jnp.dot(h_ref[...], w_ref[...], preferred_element_type=F32)
    scale = jnp.where(j == Q_COL_BLOCK, QK_SCALE, 1.0).astype(F32)
    z_ref[...] = (z * scale).astype(BF16)
    u_ref[...] = z


def _inproj(layer, x, g, w):
    t = x.shape[0]
    grid = (t // IN_TM, IN_WIDTH // IN_TN)
    return pl.pallas_call(
        _inproj_kernel,
        grid=grid,
        in_specs=[
            pl.BlockSpec((IN_TM, D_MODEL), lambda i, j: (i, 0)),
            pl.BlockSpec((None, 1, D_MODEL), lambda i, j: (layer, 0, 0)),
            pl.BlockSpec((None, D_MODEL, IN_TN), lambda i, j: (layer, 0, j)),
        ],
        out_specs=[
            pl.BlockSpec((IN_TM, IN_TN), lambda i, j: (i, j)),
            pl.BlockSpec((IN_TM, POOL_WIDTH), lambda i, j: (i, 0)),
        ],
        out_shape=[
            jax.ShapeDtypeStruct((t, IN_WIDTH), BF16),
            jax.ShapeDtypeStruct((t, POOL_WIDTH), F32),
        ],
        scratch_shapes=[pltpu.VMEM((IN_TM, D_MODEL), BF16)],
        compiler_params=pltpu.CompilerParams(
            dimension_semantics=("arbitrary", "arbitrary"),
            vmem_limit_bytes=VMEM_LIMIT_BYTES),
        name="inproj",
    )(x, g, w)


def _band_bias(table):
    n_heads = table.shape[0]
    band_keys = (LEFT_CHUNKS + 1) * CHUNK
    far = band_keys - REL_CLIP
    near = REL_CLIP + CHUNK - 1
    table = table.astype(F32)
    ext = jnp.concatenate(
        [jnp.broadcast_to(table[:, 2 * REL_CLIP:], (n_heads, far)),
         table[:, 2 * REL_CLIP - near:2 * REL_CLIP][:, ::-1]], axis=1)
    rows = jnp.stack([ext[:, CHUNK - 1 - qi:CHUNK - 1 - qi + band_keys] for qi in range(CHUNK)], axis=1)
    rows = rows * LOG2_E
    pad = ATT_BK - band_keys
    masked = jnp.full((n_heads, CHUNK, pad), MASK_VALUE, F32)
    blocks = [jnp.concatenate([masked[:, :, :c * CHUNK], rows, masked[:, :, :pad - c * CHUNK]], axis=2)
              for c in range(ATT_BQ // CHUNK)]
    bias = jnp.concatenate(blocks, axis=1)
    key_block = lax.broadcasted_iota(jnp.int32, (ATT_NKB, 1, 1, ATT_BK), 3) // ATT_BQ
    n_masked = lax.broadcasted_iota(jnp.int32, (ATT_NKB, 1, 1, ATT_BK), 0)
    return jnp.where(key_block < n_masked, F32(MASK_VALUE), bias[None])


def _attn_kernel(kvq_ref, bias_ref, o_ref, kv_ref, p_ref, linv_ref):
    i = pl.program_id(0)

    @pl.when(i == 0)
    def _():
        kv_ref[...] = jnp.zeros_like(kv_ref)
        p_ref[...] = jnp.zeros_like(p_ref)
        linv_ref[...] = jnp.zeros_like(linv_ref)

    kv_ref[0:ATT_BK, :] = kv_ref[ATT_BQ:, :]
    kv_ref[ATT_BK:, :] = kvq_ref[:, 0:2 * ATTN_WIDTH]
    cur = i % 2
    prev = 1 - cur

    for h in range(N_HEADS):
        hs = slice(h * HEAD_DIM, (h + 1) * HEAD_DIM)
        qs = slice(2 * ATTN_WIDTH + h * HEAD_DIM, 2 * ATTN_WIDTH + (h + 1) * HEAD_DIM)
        s = lax.dot_general(kvq_ref[:, qs], kv_ref[ATT_BQ:, hs], (((1,), (1,)), ((), ())),
                            preferred_element_type=F32)
        s = s + bias_ref[h]
        m = jnp.max(s, axis=-1, keepdims=True)
        p = jnp.exp2(s - m)
        l = jnp.sum(p, axis=-1, keepdims=True)
        p_ref[cur, h] = p.astype(BF16)
        linv_ref[cur, h] = jnp.broadcast_to(1.0 / l, (ATT_BQ, HEAD_DIM))

    for h in range(N_HEADS):
        vs = slice(ATTN_WIDTH + h * HEAD_DIM, ATTN_WIDTH + (h + 1) * HEAD_DIM)
        o = jnp.dot(p_ref[prev, h], kv_ref[0:ATT_BK, vs], preferred_element_type=F32)
        o_ref[:, h * HEAD_DIM:(h + 1) * HEAD_DIM] = (o * linv_ref[prev, h]).astype(BF16)


def _attention(z, bias, seq_len):
    t = z.shape[0]
    n_blocks = t // ATT_BQ
    blocks_per_seq = seq_len // ATT_BQ

    def cur_block(i):
        return jnp.minimum(i, n_blocks - 1)

    def bias_map(i):
        return (jnp.maximum(ATT_NKB - 1 - cur_block(i) % blocks_per_seq, 0), 0, 0, 0)

    return pl.pallas_call(
        _attn_kernel,
        grid=(n_blocks + 1,),
        in_specs=[pl.BlockSpec((ATT_BQ, 3 * ATTN_WIDTH), lambda i: (cur_block(i), 0)),
                  pl.BlockSpec((None, N_HEADS, ATT_BQ, ATT_BK), bias_map)],
        out_specs=pl.BlockSpec((ATT_BQ, ATTN_WIDTH), lambda i: (jnp.maximum(i - 1, 0), 0)),
        out_shape=jax.ShapeDtypeStruct((t, ATTN_WIDTH), BF16),
        scratch_shapes=[pltpu.VMEM((ATT_BK + ATT_BQ, 2 * ATTN_WIDTH), BF16),
                        pltpu.VMEM((2, N_HEADS, ATT_BQ, ATT_BK), BF16),
                        pltpu.VMEM((2, N_HEADS, ATT_BQ, HEAD_DIM), F32)],
        compiler_params=pltpu.CompilerParams(
            dimension_semantics=("arbitrary",),
            vmem_limit_bytes=VMEM_LIMIT_BYTES),
        name="attention",
    )(z, bias)


def _mixout_kernel(tiles_per_seq, x_ref, a_ref, u_ref, uh_ref, pw_ref, ps_ref, woa_ref, wop_ref,
                   g_ref, o_ref, ebuf_ref, p_ref, m_ref):
    tm = MIX_TM
    sub = tm // MIX_ROW_SPLIT
    i_in_seq = pl.program_id(0) % tiles_per_seq

    ebuf_ref[0:POOL_HALO, :] = jnp.where(i_in_seq == 0, 0.0, uh_ref[...])
    ebuf_ref[POOL_HALO:, :] = u_ref[...]

    n_groups = len(POOL_WINDOWS)
    a_sub = tm // n_groups
    head_pos = i_in_seq * tm + lax.broadcasted_iota(jnp.int32, (POOL_HALO, POOL_GROUP), 0)
    for g, w in enumerate(POOL_WINDOWS):
        rows = slice(g * a_sub, (g + 1) * a_sub)
        m_ref[rows, :] = jnp.dot(a_ref[rows, :], woa_ref[...], preferred_element_type=F32)

        cs = slice(g * POOL_GROUP, (g + 1) * POOL_GROUP)
        tok = ebuf_ref[POOL_HALO:, cs]
        wsum = tok
        for k in range(1, w):
            wsum = wsum + ebuf_ref[POOL_HALO - k:POOL_HALO - k + tm, cs]
        body = wsum * (1.0 / w) - tok
        head_count = jnp.minimum(head_pos + 1, w).astype(F32)
        head = wsum[:POOL_HALO] / head_count - tok[:POOL_HALO]
        pooled = jnp.concatenate([head, body[POOL_HALO:]], axis=0).astype(BF16)
        mixed = jnp.dot(pooled, pw_ref[g], preferred_element_type=F32)
        p_ref[:, cs] = (mixed * ps_ref[:, cs]).astype(BF16)

    for r in range(MIX_ROW_SPLIT):
        rows = slice(r * sub, (r + 1) * sub)
        m = m_ref[rows, :] + jnp.dot(p_ref[rows, :], wop_ref[...], preferred_element_type=F32)
        o_ref[rows, :] = x_ref[rows, :] + (m * _rms_scale(m)) * g_ref[...]


def _mixout(layer, x, a, u, pool_w, pool_scale, w_o, g, seq_len):
    t = x.shape[0]
    tm = MIX_TM
    tiles_per_seq = seq_len // tm
    halo_blocks = tm // POOL_HALO
    n_groups = len(POOL_WINDOWS)
    return pl.pallas_call(
        functools.partial(_mixout_kernel, tiles_per_seq),
        grid=(t // tm,),
        in_specs=[
            pl.BlockSpec((tm, D_MODEL), lambda i: (i, 0)),
            pl.BlockSpec((tm, ATTN_WIDTH), lambda i: (i, 0)),
            pl.BlockSpec((tm, POOL_WIDTH), lambda i: (i, 0)),
            pl.BlockSpec((POOL_HALO, POOL_WIDTH), lambda i: (jnp.maximum(i * halo_blocks - 1, 0), 0)),
            pl.BlockSpec((None, n_groups, POOL_GROUP, POOL_GROUP), lambda i: (layer, 0, 0, 0)),
            pl.BlockSpec((None, 1, POOL_WIDTH), lambda i: (layer, 0, 0)),
            pl.BlockSpec((None, ATTN_WIDTH, D_MODEL), lambda i: (layer, 0, 0)),
            pl.BlockSpec((None, POOL_WIDTH, D_MODEL), lambda i: (layer, 1, 0)),
            pl.BlockSpec((None, 1, D_MODEL), lambda i: (layer, 0, 0)),
        ],
        out_specs=pl.BlockSpec((tm, D_MODEL), lambda i: (i, 0)),
        out_shape=jax.ShapeDtypeStruct((t, D_MODEL), F32),
        scratch_shapes=[pltpu.VMEM((tm + POOL_HALO, POOL_WIDTH), F32),
                        pltpu.VMEM((tm, POOL_WIDTH), BF16),
                        pltpu.VMEM((tm, D_MODEL), F32)],
        compiler_params=pltpu.CompilerParams(
            dimension_semantics=("arbitrary",),
            vmem_limit_bytes=VMEM_LIMIT_BYTES),
        name="mixout",
    )(x, a, u, u, pool_w, pool_scale, w_o, w_o, g)


def _gelu_tanh(x):
    return 0.5 * x * (1.0 + jnp.tanh(math.sqrt(2.0 / math.pi) * (x + 0.044715 * (x * x * x))))


def _ffn_kernel(tiles_per_seq, x_ref, g_ref, wg_ref, wv_ref, cwg_ref, cwv_ref, cbg_ref, cbv_ref,
                wd_ref, pg_ref, o_ref, h_ref, ug_ref, uv_ref, carry_g_ref, carry_v_ref):
    tm = FFN_TM
    i = pl.program_id(0)
    j = pl.program_id(1)
    first_in_seq = (i % tiles_per_seq) == 0
    y_ref = o_ref

    @pl.when(j == 0)
    def _():
        x = x_ref[...]
        h_ref[...] = ((x * _rms_scale(x)) * g_ref[...]).astype(BF16)
        y_ref[...] = jnp.zeros_like(y_ref)

    sub = tm // FFN_ROW_SPLIT

    def up(s, w_ref, buf_ref):
        rows = slice(s * sub, (s + 1) * sub)
        buf_ref[CONV_HALO + s * sub:CONV_HALO + (s + 1) * sub, :] = jnp.dot(
            h_ref[rows, :], w_ref[...], preferred_element_type=F32)

    def conv(s, cw_ref, cb_ref, buf_ref):
        out = cb_ref[...]
        for tap in range(CONV_WIDTH):
            start = CONV_HALO - (CONV_WIDTH - 1) + tap + s * sub
            out = out + buf_ref[start:start + sub, :] * cw_ref[tap:tap + 1, :]
        return out

    ug_ref[0:CONV_HALO, :] = jnp.where(first_in_seq, 0.0, carry_g_ref[j])
    uv_ref[0:CONV_HALO, :] = jnp.where(first_in_seq, 0.0, carry_v_ref[j])
    for s in range(FFN_ROW_SPLIT):
        up(s, wg_ref, ug_ref)
        up(s, wv_ref, uv_ref)
    carry_g_ref[j] = ug_ref[tm:tm + CONV_HALO, :]
    carry_v_ref[j] = uv_ref[tm:tm + CONV_HALO, :]
    for s in range(FFN_ROW_SPLIT):
        rows = slice(s * sub, (s + 1) * sub)
        gate = conv(s, cwg_ref, cbg_ref, ug_ref)
        val = conv(s, cwv_ref, cbv_ref, uv_ref)
        act = (_gelu_tanh(gate) * val).astype(BF16)
        y_ref[rows, :] += jnp.dot(act, wd_ref[...], preferred_element_type=F32)

    @pl.when(j == pl.num_programs(1) - 1)
    def _():
        yy = y_ref[...]
        o_ref[...] = x_ref[...] + (yy * _rms_scale(yy)) * pg_ref[...]


def _ffn(layer, x, g, w_up, conv_w, conv_b, w_down, post_g, seq_len):
    t = x.shape[0]
    tm, tf = FFN_TM, FFN_TF
    n_f = D_FF // tf
    tiles_per_seq = seq_len // tm
    return pl.pallas_call(
        functools.partial(_ffn_kernel, tiles_per_seq),
        grid=(t // tm, n_f),
        in_specs=[
            pl.BlockSpec((tm, D_MODEL), lambda i, j: (i, 0)),
            pl.BlockSpec((None, 1, D_MODEL), lambda i, j: (layer, 0, 0)),
            pl.BlockSpec((None, D_MODEL, tf), lambda i, j: (layer, 0, j)),
            pl.BlockSpec((None, D_MODEL, tf), lambda i, j: (layer, 0, j + n_f)),
            pl.BlockSpec((None, CONV_WIDTH, tf), lambda i, j: (layer, 0, j)),
            pl.BlockSpec((None, CONV_WIDTH, tf), lambda i, j: (layer, 0, j + n_f)),
            pl.BlockSpec((None, 1, tf), lambda i, j: (layer, 0, j)),
            pl.BlockSpec((None, 1, tf), lambda i, j: (layer, 0, j + n_f)),
            pl.BlockSpec((None, tf, D_MODEL), lambda i, j: (layer, j, 0)),
            pl.BlockSpec((None, 1, D_MODEL), lambda i, j: (layer, 0, 0)),
        ],
        out_specs=pl.BlockSpec((tm, D_MODEL), lambda i, j: (i, 0)),
        out_shape=jax.ShapeDtypeStruct((t, D_MODEL), F32),
        scratch_shapes=[
            pltpu.VMEM((tm, D_MODEL), BF16),
            pltpu.VMEM((tm + CONV_HALO, tf), F32),
            pltpu.VMEM((tm + CONV_HALO, tf), F32),
            pltpu.VMEM((n_f, CONV_HALO, tf), F32),
            pltpu.VMEM((n_f, CONV_HALO, tf), F32),
        ],
        compiler_params=pltpu.CompilerParams(
            dimension_semantics=("arbitrary", "arbitrary"),
            vmem_limit_bytes=VMEM_LIMIT_BYTES),
        name="ffn",
    )(x, g, w_up, w_up, conv_w, conv_w, conv_b, conv_b, w_down, post_g)


def kernel(x, pre_mix_g, w_in, rel_bias, pool_w, pool_scale, w_o, post_mix_g,
           pre_ffn_g, w_up, conv_w, conv_b, w_down, post_ffn_g):
    b, s, d = x.shape
    depth = w_in.shape[0]
    w_in = jnp.concatenate([w_in[..., ATTN_WIDTH:3 * ATTN_WIDTH], w_in[..., :ATTN_WIDTH],
                            w_in[..., 3 * ATTN_WIDTH:]], axis=-1)
    w_in, pool_w, w_o, w_up, w_down = (w.astype(BF16) for w in (w_in, pool_w, w_o, w_up, w_down))
    pre_mix_g, pool_scale, post_mix_g, pre_ffn_g, conv_b, post_ffn_g = (
        _row(v) for v in (pre_mix_g, pool_scale, post_mix_g, pre_ffn_g, conv_b, post_ffn_g))
    xt = x.reshape(b * s, d)
    for l in range(depth):
        z, u = _inproj(l, xt, pre_mix_g, w_in)
        a = _attention(z, _band_bias(rel_bias[l]), s)
        xt = _mixout(l, xt, a, u, pool_w, pool_scale, w_o, post_mix_g, s)
        xt = _ffn(l, xt, pre_ffn_g, w_up, conv_w, conv_b, w_down, post_ffn_g, s)
    return xt.reshape(b, s, d)
```

```python
import functools
import math

import jax
import jax.numpy as jnp
from jax import lax
from jax.experimental import pallas as pl
from jax.experimental.pallas import tpu as pltpu

F32 = jnp.float32
BF16 = jnp.bfloat16

D_MODEL = 2048
CHUNK = 64
LEFT_CHUNKS = 8
ATTN_WIDTH = 1024
POOL_WIDTH = 1024
HEAD_DIM = 128
N_HEADS = ATTN_WIDTH // HEAD_DIM
REL_CLIP = 128
POOL_WINDOWS = (2, 4, 8, 16)
POOL_GROUP = POOL_WIDTH // len(POOL_WINDOWS)
IN_WIDTH = 3 * ATTN_WIDTH + POOL_WIDTH
D_FF = 5632
CONV_WIDTH = 3
NORM_EPS = 1e-6
MASK_VALUE = -1e30
LOG2_E = math.log2(math.e)
QK_SCALE = LOG2_E / math.sqrt(HEAD_DIM)
KV_COL_BLOCK = 0
Q_COL_BLOCK = 2
U_COL_BLOCK = 3

V7X_VMEM_BYTES = 64 * 1024 * 1024
VMEM_LIMIT_BYTES = V7X_VMEM_BYTES - 6 * 1024 * 1024
V7X_MXU_COLS = 256

IN_TM = 1024
IN_TN = 1024
ATT_BQ = 2 * CHUNK
ATT_NKB = LEFT_CHUNKS * CHUNK // ATT_BQ + 1
ATT_BK = ATT_NKB * ATT_BQ
MIX_TM = 512
MIX_ROW_SPLIT = 2
POOL_HALO = 16
FFN_TM = 512
FFN_TF = 512
FFN_ROW_SPLIT = 2
CONV_HALO = 8


def _rms_scale(x):
    return lax.rsqrt(jnp.mean(x * x, axis=-1, keepdims=True) + NORM_EPS)


def _row(vec):
    return vec.reshape(vec.shape[0], 1, vec.shape[1])


def _inproj_kernel(x_ref, g_ref, w_ref, z_ref, u_ref, h_ref):
    j = pl.program_id(1)

    @pl.when(j == 0)
    def _():
        x = x_ref[...]
        h_ref[...] = ((x * _rms_scale(x)) * g_ref[...]).astype(BF16)

    z = jnp.dot(h_ref[...], w_ref[...], preferred_element_type=F32)
    scale = jnp.where(j == Q_COL_BLOCK, QK_SCALE, 1.0).astype(F32)
    z_ref[...] = (z * scale).astype(BF16)
    u_ref[...] = z


def _inproj(layer, x, g, w):
    t = x.shape[0]
    grid = (t // IN_TM, IN_WIDTH // IN_TN)
    return pl.pallas_call(
        _inproj_kernel,
        grid=grid,
        in_specs=[
            pl.BlockSpec((IN_TM, D_MODEL), lambda i, j: (i, 0)),
            pl.BlockSpec((None, 1, D_MODEL), lambda i, j: (layer, 0, 0)),
            pl.BlockSpec((None, D_MODEL, IN_TN), lambda i, j: (layer, 0, j)),
        ],
        out_specs=[
            pl.BlockSpec((IN_TM, IN_TN), lambda i, j: (i, j)),
            pl.BlockSpec((IN_TM, POOL_WIDTH), lambda i, j: (i, 0)),
        ],
        out_shape=[
            jax.ShapeDtypeStruct((t, IN_WIDTH), BF16),
            jax.ShapeDtypeStruct((t, POOL_WIDTH), F32),
        ],
        scratch_shapes=[pltpu.VMEM((IN_TM, D_MODEL), BF16)],
        compiler_params=pltpu.CompilerParams(
            dimension_semantics=("arbitrary", "arbitrary"),
            vmem_limit_bytes=VMEM_LIMIT_BYTES),
        name="inproj",
    )(x, g, w)


def _band_bias(table):
    n_heads = table.shape[0]
    band_keys = (LEFT_CHUNKS + 1) * CHUNK
    far = band_keys - REL_CLIP
    near = REL_CLIP + CHUNK - 1
    table = table.astype(F32)
    ext = jnp.concatenate(
        [jnp.broadcast_to(table[:, 2 * REL_CLIP:], (n_heads, far)),
         table[:, 2 * REL_CLIP - near:2 * REL_CLIP][:, ::-1]], axis=1)
    rows = jnp.stack([ext[:, CHUNK - 1 - qi:CHUNK - 1 - qi + band_keys] for qi in range(CHUNK)], axis=1)
    rows = rows * LOG2_E
    pad = ATT_BK - band_keys
    masked = jnp.full((n_heads, CHUNK, pad), MASK_VALUE, F32)
    blocks = [jnp.concatenate([masked[:, :, :c * CHUNK], rows, masked[:, :, :pad - c * CHUNK]], axis=2)
              for c in range(ATT_BQ // CHUNK)]
    bias = jnp.concatenate(blocks, axis=1)
    key_block = lax.broadcasted_iota(jnp.int32, (ATT_NKB, 1, 1, ATT_BK), 3) // ATT_BQ
    n_masked = lax.broadcasted_iota(jnp.int32, (ATT_NKB, 1, 1, ATT_BK), 0)
    return jnp.where(key_block < n_masked, F32(MASK_VALUE), bias[None])


def _attn_kernel(n_blocks, q_ref, z_hbm_ref, bias_ref, o_ref, kv_ref, p_ref, linv_ref, sem):
    i = pl.program_id(0)
    last = pl.num_programs(0) - 1
    cur = i % 2
    nxt = 1 - cur
    win_blocks = ATT_NKB + 1

    def window_copies(block, slot, act):
        block = jnp.minimum(block, n_blocks - 1)
        for nb in range(1, win_blocks + 1):
            cond = (block == nb - 1) if nb < win_blocks else (block >= nb - 1)
            rows = nb * ATT_BQ

            @pl.when(cond)
            def _(rows=rows):
                first = pl.multiple_of((block + 1) * ATT_BQ - rows, ATT_BQ)
                act(pltpu.make_async_copy(
                    z_hbm_ref.at[pl.ds(first, rows), pl.ds(0, 2 * ATTN_WIDTH)],
                    kv_ref.at[slot, pl.ds(win_blocks * ATT_BQ - rows, rows)], sem.at[slot]))

    @pl.when(i == 0)
    def _():
        kv_ref[...] = jnp.zeros_like(kv_ref)
        p_ref[...] = jnp.zeros_like(p_ref)
        linv_ref[...] = jnp.zeros_like(linv_ref)
        window_copies(0, 0, lambda c: c.start())

    window_copies(i, cur, lambda c: c.wait())

    @pl.when(i < last)
    def _():
        window_copies(i + 1, nxt, lambda c: c.start())

    lane_blocks = ATT_BK // HEAD_DIM
    for h in range(N_HEADS):
        hs = slice(h * HEAD_DIM, (h + 1) * HEAD_DIM)
        s = lax.dot_general(q_ref[:, hs], kv_ref[cur, ATT_BQ:, hs], (((1,), (1,)), ((), ())),
                            preferred_element_type=F32)
        s = s + bias_ref[h]
        m = jnp.max(s, axis=-1, keepdims=True)
        p_sum = None
        for c in range(lane_blocks):
            cols = slice(c * HEAD_DIM, (c + 1) * HEAD_DIM)
            p = jnp.exp2(s[:, cols] - m)
            p_ref[cur, h, :, cols] = p.astype(BF16)
            p_sum = p if p_sum is None else p_sum + p
        l = jnp.sum(p_sum, axis=-1, keepdims=True)
        linv_ref[cur, h] = jnp.broadcast_to(1.0 / l, (ATT_BQ, HEAD_DIM))

    v_rows = pl.ds(pl.multiple_of(jnp.where(i == last, ATT_BQ, 0), ATT_BQ), ATT_BK)
    for h in range(N_HEADS):
        vs = slice(ATTN_WIDTH + h * HEAD_DIM, ATTN_WIDTH + (h + 1) * HEAD_DIM)
        o = jnp.dot(p_ref[nxt, h], kv_ref[cur, v_rows, vs], preferred_element_type=F32)
        o_ref[:, h * HEAD_DIM:(h + 1) * HEAD_DIM] = (o * linv_ref[nxt, h]).astype(BF16)


def _attention(z, bias, seq_len):
    t = z.shape[0]
    n_blocks = t // ATT_BQ
    blocks_per_seq = seq_len // ATT_BQ

    def cur_block(i):
        return jnp.minimum(i, n_blocks - 1)

    def bias_map(i):
        return (jnp.maximum(ATT_NKB - 1 - cur_block(i) % blocks_per_seq, 0), 0, 0, 0)

    return pl.pallas_call(
        functools.partial(_attn_kernel, n_blocks),
        grid=(n_blocks + 1,),
        in_specs=[pl.BlockSpec((ATT_BQ, ATTN_WIDTH), lambda i: (cur_block(i), Q_COL_BLOCK)),
                  pl.BlockSpec(memory_space=pl.ANY),
                  pl.BlockSpec((None, N_HEADS, ATT_BQ, ATT_BK), bias_map)],
        out_specs=pl.BlockSpec((ATT_BQ, ATTN_WIDTH), lambda i: (jnp.maximum(i - 1, 0), 0)),
        out_shape=jax.ShapeDtypeStruct((t, ATTN_WIDTH), BF16),
        scratch_shapes=[pltpu.VMEM((2, ATT_BK + ATT_BQ, 2 * ATTN_WIDTH), BF16),
                        pltpu.VMEM((2, N_HEADS, ATT_BQ, ATT_BK), BF16),
                        pltpu.VMEM((2, N_HEADS, ATT_BQ, HEAD_DIM), F32),
                        pltpu.SemaphoreType.DMA((2,))],
        compiler_params=pltpu.CompilerParams(
            dimension_semantics=("arbitrary",),
            vmem_limit_bytes=VMEM_LIMIT_BYTES),
        name="attention",
    )(z, z, bias)


def _mixout_kernel(tiles_per_seq, x_ref, a_ref, u_ref, uh_ref, pw_ref, ps_ref, woa_ref, wop_ref,
                   g_ref, o_ref, ebuf_ref, p_ref, m_ref):
    tm = MIX_TM
    sub = tm // MIX_ROW_SPLIT
    i_in_seq = pl.program_id(0) % tiles_per_seq

    ebuf_ref[0:POOL_HALO, :] = jnp.where(i_in_seq == 0, 0.0, uh_ref[...])
    ebuf_ref[POOL_HALO:, :] = u_ref[...]

    n_groups = len(POOL_WINDOWS)
    a_sub = tm // n_groups
    head_pos = i_in_seq * tm + lax.broadcasted_iota(jnp.int32, (POOL_HALO, POOL_GROUP), 0)
    for g, w in enumerate(POOL_WINDOWS):
        rows = slice(g * a_sub, (g + 1) * a_sub)
        m_ref[rows, :] = jnp.dot(a_ref[rows, :], woa_ref[...], preferred_element_type=F32)

        cs = slice(g * POOL_GROUP, (g + 1) * POOL_GROUP)
        tok = ebuf_ref[POOL_HALO:, cs]
        wsum = tok
        for k in range(1, w):
            wsum = wsum + ebuf_ref[POOL_HALO - k:POOL_HALO - k + tm, cs]
        body = wsum * (1.0 / w) - tok
        head_count = jnp.minimum(head_pos + 1, w).astype(F32)
        head = wsum[:POOL_HALO] / head_count - tok[:POOL_HALO]
        pooled = jnp.concatenate([head, body[POOL_HALO:]], axis=0).astype(BF16)
        mixed = jnp.dot(pooled, pw_ref[g], preferred_element_type=F32)
        p_ref[:, cs] = (mixed * ps_ref[:, cs]).astype(BF16)

    for r in range(MIX_ROW_SPLIT):
        rows = slice(r * sub, (r + 1) * sub)
        m = m_ref[rows, :] + jnp.dot(p_ref[rows, :], wop_ref[...], preferred_element_type=F32)
        o_ref[rows, :] = x_ref[rows, :] + (m * _rms_scale(m)) * g_ref[...]


def _mixout(layer, x, a, u, pool_w, pool_scale, w_o, g, seq_len):
    t = x.shape[0]
    tm = MIX_TM
    tiles_per_seq = seq_len // tm
    halo_blocks = tm // POOL_HALO
    n_groups = len(POOL_WINDOWS)
    return pl.pallas_call(
        functools.partial(_mixout_kernel, tiles_per_seq),
        grid=(t // tm,),
        in_specs=[
            pl.BlockSpec((tm, D_MODEL), lambda i: (i, 0)),
            pl.BlockSpec((tm, ATTN_WIDTH), lambda i: (i, 0)),
            pl.BlockSpec((tm, POOL_WIDTH), lambda i: (i, 0)),
            pl.BlockSpec((POOL_HALO, POOL_WIDTH), lambda i: (jnp.maximum(i * halo_blocks - 1, 0), 0)),
            pl.BlockSpec((None, n_groups, POOL_GROUP, POOL_GROUP), lambda i: (layer, 0, 0, 0)),
            pl.BlockSpec((None, 1, POOL_WIDTH), lambda i: (layer, 0, 0)),
            pl.BlockSpec((None, ATTN_WIDTH, D_MODEL), lambda i: (layer, 0, 0)),
            pl.BlockSpec((None, POOL_WIDTH, D_MODEL), lambda i: (layer, 1, 0)),
            pl.BlockSpec((None, 1, D_MODEL), lambda i: (layer, 0, 0)),
        ],
        out_specs=pl.BlockSpec((tm, D_MODEL), lambda i: (i, 0)),
        out_shape=jax.ShapeDtypeStruct((t, D_MODEL), F32),
        scratch_shapes=[pltpu.VMEM((tm + POOL_HALO, POOL_WIDTH), F32),
                        pltpu.VMEM((tm, POOL_WIDTH), BF16),
                        pltpu.VMEM((tm, D_MODEL), F32)],
        compiler_params=pltpu.CompilerParams(
            dimension_semantics=("arbitrary",),
            vmem_limit_bytes=VMEM_LIMIT_BYTES),
        name="mixout",
    )(x, a, u, u, pool_w, pool_scale, w_o, w_o, g)


def _gelu_tanh(x):
    return 0.5 * x * (1.0 + jnp.tanh(math.sqrt(2.0 / math.pi) * (x + 0.044715 * (x * x * x))))


def _pair_gate_value_chunks(v):
    lead = v.shape[:-1]
    v = v.reshape(lead + (2, D_FF // FFN_TF, FFN_TF))
    return jnp.swapaxes(v, -3, -2).reshape(lead + (2 * D_FF,))


def _ffn_kernel(tiles_per_seq, x_ref, g_ref, wu_ref, cw_ref, cb_ref, wd_ref, pg_ref, o_ref,
                h_ref, u_ref, carry_ref):
    tm = FFN_TM
    tf = FFN_TF
    i = pl.program_id(0)
    j = pl.program_id(1)
    first_in_seq = (i % tiles_per_seq) == 0
    y_ref = o_ref

    @pl.when(j == 0)
    def _():
        x = x_ref[...]
        h_ref[...] = ((x * _rms_scale(x)) * g_ref[...]).astype(BF16)
        y_ref[...] = jnp.zeros_like(y_ref)

    sub = tm // FFN_ROW_SPLIT

    u_ref[0:CONV_HALO, :] = jnp.where(first_in_seq, 0.0, carry_ref[j])
    for r in range(FFN_ROW_SPLIT):
        u_ref[CONV_HALO + r * sub:CONV_HALO + (r + 1) * sub, :] = jnp.dot(
            h_ref[r * sub:(r + 1) * sub, :], wu_ref[...], preferred_element_type=F32)
    carry_ref[j] = u_ref[tm:tm + CONV_HALO, :]
    for r in range(FFN_ROW_SPLIT):
        rows = slice(r * sub, (r + 1) * sub)
        conv = cb_ref[...]
        for tap in range(CONV_WIDTH):
            start = CONV_HALO - (CONV_WIDTH - 1) + tap + r * sub
            conv = conv + u_ref[start:start + sub, :] * cw_ref[tap:tap + 1, :]
        act = (_gelu_tanh(conv[:, :tf]) * conv[:, tf:]).astype(BF16)
        y_ref[rows, :] += jnp.dot(act, wd_ref[...], preferred_element_type=F32)

    @pl.when(j == pl.num_programs(1) - 1)
    def _():
        yy = y_ref[...]
        o_ref[...] = x_ref[...] + (yy * _rms_scale(yy)) * pg_ref[...]


def _ffn(layer, x, g, w_up, conv_w, conv_b, w_down, post_g, seq_len):
    t = x.shape[0]
    tm, tf = FFN_TM, FFN_TF
    n_f = D_FF // tf
    tiles_per_seq = seq_len // tm
    return pl.pallas_call(
        functools.partial(_ffn_kernel, tiles_per_seq),
        grid=(t // tm, n_f),
        in_specs=[
            pl.BlockSpec((tm, D_MODEL), lambda i, j: (i, 0)),
            pl.BlockSpec((None, 1, D_MODEL), lambda i, j: (layer, 0, 0)),
            pl.BlockSpec((None, D_MODEL, 2 * tf), lambda i, j: (layer, 0, j)),
            pl.BlockSpec((None, CONV_WIDTH, 2 * tf), lambda i, j: (layer, 0, j)),
            pl.BlockSpec((None, 1, 2 * tf), lambda i, j: (layer, 0, j)),
            pl.BlockSpec((None, tf, D_MODEL), lambda i, j: (layer, j, 0)),
            pl.BlockSpec((None, 1, D_MODEL), lambda i, j: (layer, 0, 0)),
        ],
        out_specs=pl.BlockSpec((tm, D_MODEL), lambda i, j: (i, 0)),
        out_shape=jax.ShapeDtypeStruct((t, D_MODEL), F32),
        scratch_shapes=[
            pltpu.VMEM((tm, D_MODEL), BF16),
            pltpu.VMEM((tm + CONV_HALO, 2 * tf), F32),
            pltpu.VMEM((n_f, CONV_HALO, 2 * tf), F32),
        ],
        compiler_params=pltpu.CompilerParams(
            dimension_semantics=("arbitrary", "arbitrary"),
            vmem_limit_bytes=VMEM_LIMIT_BYTES),
        name="ffn",
    )(x, g, w_up, conv_w, conv_b, w_down, post_g)


def kernel(x, pre_mix_g, w_in, rel_bias, pool_w, pool_scale, w_o, post_mix_g,
           pre_ffn_g, w_up, conv_w, conv_b, w_down, post_ffn_g):
    b, s, d = x.shape
    depth = w_in.shape[0]
    w_in = jnp.concatenate([w_in[..., ATTN_WIDTH:3 * ATTN_WIDTH], w_in[..., :ATTN_WIDTH],
                            w_in[..., 3 * ATTN_WIDTH:]], axis=-1)
    w_up, conv_w, conv_b = (_pair_gate_value_chunks(v) for v in (w_up, conv_w, conv_b))
    w_in, pool_w, w_o, w_up, w_down = (w.astype(BF16) for w in (w_in, pool_w, w_o, w_up, w_down))
    pre_mix_g, pool_scale, post_mix_g, pre_ffn_g, conv_b, post_ffn_g = (
        _row(v) for v in (pre_mix_g, pool_scale, post_mix_g, pre_ffn_g, conv_b, post_ffn_g))
    xt = x.reshape(b * s, d)
    for l in range(depth):
        z, u = _inproj(l, xt, pre_mix_g, w_in)
        a = _attention(z, _band_bias(rel_bias[l]), s)
        xt = _mixout(l, xt, a, u, pool_w, pool_scale, w_o, post_mix_g, s)
        xt = _ffn(l, xt, pre_ffn_g, w_up, conv_w, conv_b, w_down, post_ffn_g, s)
    return xt.reshape(b, s, d)
```

```python
import functools
import math

import jax
import jax.numpy as jnp
from jax import lax
from jax.experimental import pallas as pl
from jax.experimental.pallas import tpu as pltpu

F32 = jnp.float32
BF16 = jnp.bfloat16

D_MODEL = 2048
CHUNK = 64
LEFT_CHUNKS = 8
ATTN_WIDTH = 1024
POOL_WIDTH = 1024
HEAD_DIM = 128
N_HEADS = ATTN_WIDTH // HEAD_DIM
REL_CLIP = 128
POOL_WINDOWS = (2, 4, 8, 16)
POOL_GROUP = POOL_WIDTH // len(POOL_WINDOWS)
IN_WIDTH = 3 * ATTN_WIDTH + POOL_WIDTH
D_FF = 5632
CONV_WIDTH = 3
NORM_EPS = 1e-6
MASK_VALUE = -1e30
LOG2_E = math.log2(math.e)
QK_SCALE = LOG2_E / math.sqrt(HEAD_DIM)
Q_COL_BLOCK = 2

V7X_VMEM_BYTES = 64 * 1024 * 1024
VMEM_LIMIT_BYTES = V7X_VMEM_BYTES - 6 * 1024 * 1024

IN_TM = 1024
IN_TN = 1024
ATT_BQ = 2 * CHUNK
ATT_NKB = LEFT_CHUNKS * CHUNK // ATT_BQ + 1
ATT_BK = ATT_NKB * ATT_BQ
ATT_RING = ATT_NKB + 1
MIX_TM = 512
MIX_ROW_SPLIT = 2
POOL_HALO = 16
FFN_TM = 512
FFN_TF = 512
FFN_ROW_SPLIT = 2
CONV_HALO = 8
NORM_ROWS = 16


def _rms_scale(x):
    return lax.rsqrt(jnp.mean(x * x, axis=-1, keepdims=True) + NORM_EPS)


def _row(vec):
    return vec.reshape(vec.shape[0], 1, vec.shape[1])


def _inproj_kernel(x_ref, g_ref, w_ref, z_ref, u_ref, h_ref):
    j = pl.program_id(1)

    @pl.when(j == 0)
    def _():
        x = x_ref[...]
        h_ref[...] = ((x * _rms_scale(x)) * g_ref[...]).astype(BF16)

    z = jnp.dot(h_ref[...], w_ref[...], preferred_element_type=F32)
    scale = jnp.where(j == Q_COL_BLOCK, QK_SCALE, 1.0).astype(F32)
    z_ref[...] = (z * scale).astype(BF16)
    u_ref[...] = z


def _inproj(layer, x, g, w):
    t = x.shape[0]
    grid = (t // IN_TM, IN_WIDTH // IN_TN)
    return pl.pallas_call(
        _inproj_kernel,
        grid=grid,
        in_specs=[
            pl.BlockSpec((IN_TM, D_MODEL), lambda i, j: (i, 0)),
            pl.BlockSpec((None, 1, D_MODEL), lambda i, j: (layer, 0, 0)),
            pl.BlockSpec((None, D_MODEL, IN_TN), lambda i, j: (layer, 0, j)),
        ],
        out_specs=[
            pl.BlockSpec((IN_TM, IN_TN), lambda i, j: (i, j)),
            pl.BlockSpec((IN_TM, POOL_WIDTH), lambda i, j: (i, 0)),
        ],
        out_shape=[
            jax.ShapeDtypeStruct((t, IN_WIDTH), BF16),
            jax.ShapeDtypeStruct((t, POOL_WIDTH), F32),
        ],
        scratch_shapes=[pltpu.VMEM((IN_TM, D_MODEL), BF16)],
        compiler_params=pltpu.CompilerParams(
            dimension_semantics=("arbitrary", "arbitrary"),
            vmem_limit_bytes=VMEM_LIMIT_BYTES),
        name="inproj",
    )(x, g, w)


def _band_bias(table):
    n_heads = table.shape[0]
    band_keys = (LEFT_CHUNKS + 1) * CHUNK
    far = band_keys - REL_CLIP
    near = REL_CLIP + CHUNK - 1
    table = table.astype(F32)
    ext = jnp.concatenate(
        [jnp.broadcast_to(table[:, 2 * REL_CLIP:], (n_heads, far)),
         table[:, 2 * REL_CLIP - near:2 * REL_CLIP][:, ::-1]], axis=1)
    rows = jnp.stack([ext[:, CHUNK - 1 - qi:CHUNK - 1 - qi + band_keys] for qi in range(CHUNK)], axis=1)
    rows = rows * LOG2_E
    pad = ATT_BK - band_keys
    masked = jnp.full((n_heads, CHUNK, pad), MASK_VALUE, F32)
    blocks = [jnp.concatenate([masked[:, :, :c * CHUNK], rows, masked[:, :, :pad - c * CHUNK]], axis=2)
              for c in range(ATT_BQ // CHUNK)]
    bias = jnp.concatenate(blocks, axis=1)
    key_block = lax.broadcasted_iota(jnp.int32, (ATT_NKB, 1, 1, ATT_BK), 3) // ATT_BQ
    n_masked = lax.broadcasted_iota(jnp.int32, (ATT_NKB, 1, 1, ATT_BK), 0)
    return jnp.where(key_block < n_masked, F32(MASK_VALUE), bias[None])


def _attn_kernel(kvq_ref, bias_ref, o_ref, kv_ref, p_ref, linv_ref):
    @pl.when(pl.program_id(0) == 0)
    def _():
        kv_ref[...] = jnp.zeros_like(kv_ref)
        p_ref[...] = jnp.zeros_like(p_ref)
        linv_ref[...] = jnp.zeros_like(linv_ref)

    kv_ref[0:ATT_BK, :] = kv_ref[ATT_BQ:ATT_BK + ATT_BQ, :]
    kv_ref[ATT_BK:ATT_BK + ATT_BQ, :] = kvq_ref[:, 0:2 * ATTN_WIDTH]
    k_rows = slice(ATT_BQ, ATT_BK + ATT_BQ)
    v_rows = slice(0, ATT_BK)

    for h in range(N_HEADS):
        vs = slice(ATTN_WIDTH + h * HEAD_DIM, ATTN_WIDTH + (h + 1) * HEAD_DIM)
        o = jnp.dot(p_ref[h], kv_ref[v_rows, vs], preferred_element_type=F32)
        o_ref[:, h * HEAD_DIM:(h + 1) * HEAD_DIM] = (o * linv_ref[h]).astype(BF16)

    for h in range(N_HEADS):
        hs = slice(h * HEAD_DIM, (h + 1) * HEAD_DIM)
        qs = slice(2 * ATTN_WIDTH + h * HEAD_DIM, 2 * ATTN_WIDTH + (h + 1) * HEAD_DIM)
        s = lax.dot_general(kvq_ref[:, qs], kv_ref[k_rows, hs], (((1,), (1,)), ((), ())),
                            preferred_element_type=F32)
        s = s + bias_ref[h]
        m = jnp.max(s, axis=-1, keepdims=True)
        p = jnp.exp2(s - m)
        l = jnp.sum(p, axis=-1, keepdims=True)
        p_ref[h] = p.astype(BF16)
        linv_ref[h] = jnp.broadcast_to(1.0 / l, (ATT_BQ, HEAD_DIM))


def _attention(z, bias, seq_len):
    t = z.shape[0]
    n_blocks = t // ATT_BQ
    blocks_per_seq = seq_len // ATT_BQ

    def cur_block(i):
        return jnp.minimum(i, n_blocks - 1)

    def bias_map(i):
        return (jnp.maximum(ATT_NKB - 1 - cur_block(i) % blocks_per_seq, 0), 0, 0, 0)

    return pl.pallas_call(
        _attn_kernel,
        grid=(n_blocks + 1,),
        in_specs=[pl.BlockSpec((ATT_BQ, 3 * ATTN_WIDTH), lambda i: (cur_block(i), 0)),
                  pl.BlockSpec((None, N_HEADS, ATT_BQ, ATT_BK), bias_map)],
        out_specs=pl.BlockSpec((ATT_BQ, ATTN_WIDTH), lambda i: (jnp.maximum(i - 1, 0), 0)),
        out_shape=jax.ShapeDtypeStruct((t, ATTN_WIDTH), BF16),
        scratch_shapes=[pltpu.VMEM((ATT_BK + ATT_BQ, 2 * ATTN_WIDTH), BF16),
                        pltpu.VMEM((N_HEADS, ATT_BQ, ATT_BK), BF16),
                        pltpu.VMEM((N_HEADS, ATT_BQ, HEAD_DIM), F32)],
        compiler_params=pltpu.CompilerParams(
            dimension_semantics=("arbitrary",),
            vmem_limit_bytes=VMEM_LIMIT_BYTES),
        name="attention",
    )(z, bias)


def _mixout_kernel(tiles_per_seq, x_ref, a_ref, u_ref, uh_ref, pw_ref, ps_ref, woa_ref, wop_ref,
                   g_ref, o_ref, ebuf_ref, p_ref, m_ref):
    tm = MIX_TM
    sub = tm // MIX_ROW_SPLIT
    i_in_seq = pl.program_id(0) % tiles_per_seq

    ebuf_ref[0:POOL_HALO, :] = jnp.where(i_in_seq == 0, 0.0, uh_ref[...])
    ebuf_ref[POOL_HALO:, :] = u_ref[...]

    n_groups = len(POOL_WINDOWS)
    a_sub = tm // n_groups
    head_pos = i_in_seq * tm + lax.broadcasted_iota(jnp.int32, (POOL_HALO, POOL_GROUP), 0)
    for g, w in enumerate(POOL_WINDOWS):
        rows = slice(g * a_sub, (g + 1) * a_sub)
        m_ref[rows, :] = jnp.dot(a_ref[rows, :], woa_ref[...], preferred_element_type=F32)

        cs = slice(g * POOL_GROUP, (g + 1) * POOL_GROUP)
        tok = ebuf_ref[POOL_HALO:, cs]
        wsum = tok
        for k in range(1, w):
            wsum = wsum + ebuf_ref[POOL_HALO - k:POOL_HALO - k + tm, cs]
        body = wsum * (1.0 / w) - tok
        head_count = jnp.minimum(head_pos + 1, w).astype(F32)
        head = wsum[:POOL_HALO] / head_count - tok[:POOL_HALO]
        pooled = jnp.concatenate([head, body[POOL_HALO:]], axis=0).astype(BF16)
        mixed = jnp.dot(pooled, pw_ref[g], preferred_element_type=F32)
        p_ref[:, cs] = (mixed * ps_ref[:, cs]).astype(BF16)

    for r in range(MIX_ROW_SPLIT):
        rows = slice(r * sub, (r + 1) * sub)
        m = m_ref[rows, :] + jnp.dot(p_ref[rows, :], wop_ref[...], preferred_element_type=F32)
        o_ref[rows, :] = x_ref[rows, :] + (m * _rms_scale(m)) * g_ref[...]


def _mixout(layer, x, a, u, pool_w, pool_scale, w_o, g, seq_len):
    t = x.shape[0]
    tm = MIX_TM
    tiles_per_seq = seq_len // tm
    halo_blocks = tm // POOL_HALO
    n_groups = len(POOL_WINDOWS)
    return pl.pallas_call(
        functools.partial(_mixout_kernel, tiles_per_seq),
        grid=(t // tm,),
        in_specs=[
            pl.BlockSpec((tm, D_MODEL), lambda i: (i, 0)),
            pl.BlockSpec((tm, ATTN_WIDTH), lambda i: (i, 0)),
            pl.BlockSpec((tm, POOL_WIDTH), lambda i: (i, 0)),
            pl.BlockSpec((POOL_HALO, POOL_WIDTH), lambda i: (jnp.maximum(i * halo_blocks - 1, 0), 0)),
            pl.BlockSpec((None, n_groups, POOL_GROUP, POOL_GROUP), lambda i: (layer, 0, 0, 0)),
            pl.BlockSpec((None, 1, POOL_WIDTH), lambda i: (layer, 0, 0)),
            pl.BlockSpec((None, ATTN_WIDTH, D_MODEL), lambda i: (layer, 0, 0)),
            pl.BlockSpec((None, POOL_WIDTH, D_MODEL), lambda i: (layer, 1, 0)),
            pl.BlockSpec((None, 1, D_MODEL), lambda i: (layer, 0, 0)),
        ],
        out_specs=pl.BlockSpec((tm, D_MODEL), lambda i: (i, 0)),
        out_shape=jax.ShapeDtypeStruct((t, D_MODEL), F32),
        scratch_shapes=[pltpu.VMEM((tm + POOL_HALO, POOL_WIDTH), F32),
                        pltpu.VMEM((tm, POOL_WIDTH), BF16),
                        pltpu.VMEM((tm, D_MODEL), F32)],
        compiler_params=pltpu.CompilerParams(
            dimension_semantics=("arbitrary",),
            vmem_limit_bytes=VMEM_LIMIT_BYTES),
        name="mixout",
    )(x, a, u, u, pool_w, pool_scale, w_o, w_o, g)


def _gelu_tanh(x):
    return 0.5 * x * (1.0 + jnp.tanh(math.sqrt(2.0 / math.pi) * (x + 0.044715 * (x * x * x))))


def _ffn_kernel(tiles_per_seq, x_ref, g_ref, wg_ref, wv_ref, cwg_ref, cwv_ref, cbg_ref, cbv_ref,
                wd_ref, pg_ref, o_ref, h_ref, ug_ref, uv_ref, carry_g_ref, carry_v_ref):
    tm = FFN_TM
    i = pl.program_id(0)
    j = pl.program_id(1)
    first_in_seq = (i % tiles_per_seq) == 0
    y_ref = o_ref

    @pl.when(j == 0)
    def _():
        x = x_ref[...]
        h_ref[...] = ((x * _rms_scale(x)) * g_ref[...]).astype(BF16)

    @pl.when(jnp.logical_and(i == 0, j == 0))
    def _():
        y_ref[...] = jnp.zeros_like(y_ref)

    sub = tm // FFN_ROW_SPLIT

    def up(r, w_ref, buf_ref):
        rows = slice(r * sub, (r + 1) * sub)
        buf_ref[CONV_HALO + r * sub:CONV_HALO + (r + 1) * sub, :] = jnp.dot(
            h_ref[rows, :], w_ref[...], preferred_element_type=F32)

    def conv(r, cw_ref, cb_ref, buf_ref):
        out = cb_ref[...]
        for tap in range(CONV_WIDTH):
            start = CONV_HALO - (CONV_WIDTH - 1) + tap + r * sub
            out = out + buf_ref[start:start + sub, :] * cw_ref[tap:tap + 1, :]
        return out

    ug_ref[0:CONV_HALO, :] = jnp.where(first_in_seq, 0.0, carry_g_ref[j])
    uv_ref[0:CONV_HALO, :] = jnp.where(first_in_seq, 0.0, carry_v_ref[j])
    for r in range(FFN_ROW_SPLIT):
        up(r, wg_ref, ug_ref)
        up(r, wv_ref, uv_ref)
    carry_g_ref[j] = ug_ref[tm:tm + CONV_HALO, :]
    carry_v_ref[j] = uv_ref[tm:tm + CONV_HALO, :]
    for r in range(FFN_ROW_SPLIT):
        rows = slice(r * sub, (r + 1) * sub)
        gate = conv(r, cwg_ref, cbg_ref, ug_ref)
        val = conv(r, cwv_ref, cbv_ref, uv_ref)
        act = (_gelu_tanh(gate) * val).astype(BF16)
        acc = jnp.where(j == 0, 0.0, y_ref[rows, :])
        y_ref[rows, :] = acc + jnp.dot(act, wd_ref[...], preferred_element_type=F32)

    @pl.when(j == pl.num_programs(1) - 1)
    def _():
        for c in range(tm // NORM_ROWS):
            rows = slice(c * NORM_ROWS, (c + 1) * NORM_ROWS)
            yy = y_ref[rows, :]
            o_ref[rows, :] = x_ref[rows, :] + (yy * _rms_scale(yy)) * pg_ref[...]


def _ffn(layer, x, g, w_up, conv_w, conv_b, w_down, post_g, seq_len):
    t = x.shape[0]
    tm, tf = FFN_TM, FFN_TF
    n_f = D_FF // tf
    tiles_per_seq = seq_len // tm
    return pl.pallas_call(
        functools.partial(_ffn_kernel, tiles_per_seq),
        grid=(t // tm, n_f),
        in_specs=[
            pl.BlockSpec((tm, D_MODEL), lambda i, j: (i, 0)),
            pl.BlockSpec((None, 1, D_MODEL), lambda i, j: (layer, 0, 0)),
            pl.BlockSpec((None, D_MODEL, tf), lambda i, j: (layer, 0, j)),
            pl.BlockSpec((None, D_MODEL, tf), lambda i, j: (layer, 0, j + n_f)),
            pl.BlockSpec((None, CONV_WIDTH, tf), lambda i, j: (layer, 0, j)),
            pl.BlockSpec((None, CONV_WIDTH, tf), lambda i, j: (layer, 0, j + n_f)),
            pl.BlockSpec((None, 1, tf), lambda i, j: (layer, 0, j)),
            pl.BlockSpec((None, 1, tf), lambda i, j: (layer, 0, j + n_f)),
            pl.BlockSpec((None, tf, D_MODEL), lambda i, j: (layer, j, 0)),
            pl.BlockSpec((None, 1, D_MODEL), lambda i, j: (layer, 0, 0)),
        ],
        out_specs=pl.BlockSpec((tm, D_MODEL), lambda i, j: (i, 0)),
        out_shape=jax.ShapeDtypeStruct((t, D_MODEL), F32),
        scratch_shapes=[
            pltpu.VMEM((tm, D_MODEL), BF16),
            pltpu.VMEM((tm + CONV_HALO, tf), F32),
            pltpu.VMEM((tm + CONV_HALO, tf), F32),
            pltpu.VMEM((n_f, CONV_HALO, tf), F32),
            pltpu.VMEM((n_f, CONV_HALO, tf), F32),
        ],
        compiler_params=pltpu.CompilerParams(
            dimension_semantics=("arbitrary", "arbitrary"),
            vmem_limit_bytes=VMEM_LIMIT_BYTES),
        name="ffn",
    )(x, g, w_up, w_up, conv_w, conv_w, conv_b, conv_b, w_down, post_g)


def kernel(x, pre_mix_g, w_in, rel_bias, pool_w, pool_scale, w_o, post_mix_g,
           pre_ffn_g, w_up, conv_w, conv_b, w_down, post_ffn_g):
    b, s, d = x.shape
    depth = w_in.shape[0]
    w_in = jnp.concatenate([w_in[..., ATTN_WIDTH:3 * ATTN_WIDTH], w_in[..., :ATTN_WIDTH],
                            w_in[..., 3 * ATTN_WIDTH:]], axis=-1)
    w_in, pool_w, w_o, w_up, w_down = (w.astype(BF16) for w in (w_in, pool_w, w_o, w_up, w_down))
    pre_mix_g, pool_scale, post_mix_g, pre_ffn_g, conv_b, post_ffn_g = (
        _row(v) for v in (pre_mix_g, pool_scale, post_mix_g, pre_ffn_g, conv_b, post_ffn_g))
    xt = x.reshape(b * s, d)
    for l in range(depth):
        z, u = _inproj(l, xt, pre_mix_g, w_in)
        a = _attention(z, _band_bias(rel_bias[l]), s)
        xt = _mixout(l, xt, a, u, pool_w, pool_scale, w_o, post_mix_g, s)
        xt = _ffn(l, xt, pre_ffn_g, w_up, conv_w, conv_b, w_down, post_ffn_g, s)
    return xt.reshape(b, s, d)
```

```python
import functools
import math

import jax
import jax.numpy as jnp
from jax import lax
from jax.experimental import pallas as pl
from jax.experimental.pallas import tpu as pltpu

F32 = jnp.float32
BF16 = jnp.bfloat16

D_MODEL = 2048
CHUNK = 64
LEFT_CHUNKS = 8
ATTN_WIDTH = 1024
POOL_WIDTH = 1024
HEAD_DIM = 128
N_HEADS = ATTN_WIDTH // HEAD_DIM
REL_CLIP = 128
POOL_WINDOWS = (2, 4, 8, 16)
POOL_GROUP = POOL_WIDTH // len(POOL_WINDOWS)
IN_WIDTH = 3 * ATTN_WIDTH + POOL_WIDTH
D_FF = 5632
CONV_WIDTH = 3
NORM_EPS = 1e-6
MASK_VALUE = -1e30
LOG2_E = math.log2(math.e)
QK_SCALE = LOG2_E / math.sqrt(HEAD_DIM)
Q_COL_BLOCK = 2

V7X_VMEM_BYTES = 64 * 1024 * 1024
VMEM_LIMIT_BYTES = V7X_VMEM_BYTES - 6 * 1024 * 1024

IN_TM = 1024
IN_TN = 1024
ATT_BQ = 4 * CHUNK
ATT_NKB = LEFT_CHUNKS * CHUNK // ATT_BQ + 1
ATT_BK = ATT_NKB * ATT_BQ
MIX_TM = 512
MIX_ROW_SPLIT = 2
POOL_HALO = 16
FFN_TM = 512
FFN_TF = 512
FFN_ROW_SPLIT = 2
CONV_HALO = 8
NORM_ROWS = 16


def _rms_scale(x):
    return lax.rsqrt(jnp.mean(x * x, axis=-1, keepdims=True) + NORM_EPS)


def _row(vec):
    return vec.reshape(vec.shape[0], 1, vec.shape[1])


def _inproj_kernel(x_ref, g_ref, w_ref, z_ref, u_ref, h_ref):
    j = pl.program_id(1)

    @pl.when(j == 0)
    def _():
        x = x_ref[...]
        h_ref[...] = ((x * _rms_scale(x)) * g_ref[...]).astype(BF16)

    z = jnp.dot(h_ref[...], w_ref[...], preferred_element_type=F32)
    scale = jnp.where(j == Q_COL_BLOCK, QK_SCALE, 1.0).astype(F32)
    z_ref[...] = (z * scale).astype(BF16)
    u_ref[...] = z


def _inproj(layer, x, g, w):
    t = x.shape[0]
    grid = (t // IN_TM, IN_WIDTH // IN_TN)
    return pl.pallas_call(
        _inproj_kernel,
        grid=grid,
        in_specs=[
            pl.BlockSpec((IN_TM, D_MODEL), lambda i, j: (i, 0)),
            pl.BlockSpec((None, 1, D_MODEL), lambda i, j: (layer, 0, 0)),
            pl.BlockSpec((None, D_MODEL, IN_TN), lambda i, j: (layer, 0, j)),
        ],
        out_specs=[
            pl.BlockSpec((IN_TM, IN_TN), lambda i, j: (i, j)),
            pl.BlockSpec((IN_TM, POOL_WIDTH), lambda i, j: (i, 0)),
        ],
        out_shape=[
            jax.ShapeDtypeStruct((t, IN_WIDTH), BF16),
            jax.ShapeDtypeStruct((t, POOL_WIDTH), F32),
        ],
        scratch_shapes=[pltpu.VMEM((IN_TM, D_MODEL), BF16)],
        compiler_params=pltpu.CompilerParams(
            dimension_semantics=("arbitrary", "arbitrary"),
            vmem_limit_bytes=VMEM_LIMIT_BYTES),
        name="inproj",
    )(x, g, w)


def _band_bias(table):
    n_heads = table.shape[0]
    band_keys = (LEFT_CHUNKS + 1) * CHUNK
    far = band_keys - REL_CLIP
    near = REL_CLIP + CHUNK - 1
    table = table.astype(F32)
    ext = jnp.concatenate(
        [jnp.broadcast_to(table[:, 2 * REL_CLIP:], (n_heads, far)),
         table[:, 2 * REL_CLIP - near:2 * REL_CLIP][:, ::-1]], axis=1)
    rows = jnp.stack([ext[:, CHUNK - 1 - qi:CHUNK - 1 - qi + band_keys] for qi in range(CHUNK)], axis=1)
    rows = rows * LOG2_E
    pad = ATT_BK - band_keys
    masked = jnp.full((n_heads, CHUNK, pad), MASK_VALUE, F32)
    blocks = [jnp.concatenate([masked[:, :, :c * CHUNK], rows, masked[:, :, :pad - c * CHUNK]], axis=2)
              for c in range(ATT_BQ // CHUNK)]
    bias = jnp.concatenate(blocks, axis=1)
    key_block = lax.broadcasted_iota(jnp.int32, (ATT_NKB, 1, 1, ATT_BK), 3) // ATT_BQ
    n_masked = lax.broadcasted_iota(jnp.int32, (ATT_NKB, 1, 1, ATT_BK), 0)
    return jnp.where(key_block < n_masked, F32(MASK_VALUE), bias[None])


def _attn_kernel(kvq_ref, bias_ref, o_ref, kv_ref, p_ref, linv_ref):
    @pl.when(pl.program_id(0) == 0)
    def _():
        kv_ref[...] = jnp.zeros_like(kv_ref)
        p_ref[...] = jnp.zeros_like(p_ref)
        linv_ref[...] = jnp.zeros_like(linv_ref)

    kv_ref[0:ATT_BK, :] = kv_ref[ATT_BQ:ATT_BK + ATT_BQ, :]
    kv_ref[ATT_BK:ATT_BK + ATT_BQ, :] = kvq_ref[:, 0:2 * ATTN_WIDTH]
    k_rows = slice(ATT_BQ, ATT_BK + ATT_BQ)
    v_rows = slice(0, ATT_BK)

    for h in range(N_HEADS):
        vs = slice(ATTN_WIDTH + h * HEAD_DIM, ATTN_WIDTH + (h + 1) * HEAD_DIM)
        o = jnp.dot(p_ref[h], kv_ref[v_rows, vs], preferred_element_type=F32)
        o_ref[:, h * HEAD_DIM:(h + 1) * HEAD_DIM] = (o * linv_ref[h]).astype(BF16)

    for h in range(N_HEADS):
        hs = slice(h * HEAD_DIM, (h + 1) * HEAD_DIM)
        qs = slice(2 * ATTN_WIDTH + h * HEAD_DIM, 2 * ATTN_WIDTH + (h + 1) * HEAD_DIM)
        s = lax.dot_general(kvq_ref[:, qs], kv_ref[k_rows, hs], (((1,), (1,)), ((), ())),
                            preferred_element_type=F32)
        s = s + bias_ref[h]
        m = jnp.max(s, axis=-1, keepdims=True)
        p = jnp.exp2(s - m)
        l = jnp.sum(p, axis=-1, keepdims=True)
        p_ref[h] = p.astype(BF16)
        linv_ref[h] = jnp.broadcast_to(1.0 / l, (ATT_BQ, HEAD_DIM))


def _attention(z, bias, seq_len):
    t = z.shape[0]
    n_blocks = t // ATT_BQ
    blocks_per_seq = seq_len // ATT_BQ

    def cur_block(i):
        return jnp.minimum(i, n_blocks - 1)

    def bias_map(i):
        return (jnp.maximum(ATT_NKB - 1 - cur_block(i) % blocks_per_seq, 0), 0, 0, 0)

    return pl.pallas_call(
        _attn_kernel,
        grid=(n_blocks + 1,),
        in_specs=[pl.BlockSpec((ATT_BQ, 3 * ATTN_WIDTH), lambda i: (cur_block(i), 0)),
                  pl.BlockSpec((None, N_HEADS, ATT_BQ, ATT_BK), bias_map)],
        out_specs=pl.BlockSpec((ATT_BQ, ATTN_WIDTH), lambda i: (jnp.maximum(i - 1, 0), 0)),
        out_shape=jax.ShapeDtypeStruct((t, ATTN_WIDTH), BF16),
        scratch_shapes=[pltpu.VMEM((ATT_BK + ATT_BQ, 2 * ATTN_WIDTH), BF16),
                        pltpu.VMEM((N_HEADS, ATT_BQ, ATT_BK), BF16),
                        pltpu.VMEM((N_HEADS, ATT_BQ, HEAD_DIM), F32)],
        compiler_params=pltpu.CompilerParams(
            dimension_semantics=("arbitrary",),
            vmem_limit_bytes=VMEM_LIMIT_BYTES),
        name="attention",
    )(z, bias)


def _mixout_kernel(tiles_per_seq, x_ref, a_ref, u_ref, uh_ref, pw_ref, ps_ref, woa_ref, wop_ref,
                   g_ref, o_ref, ebuf_ref, p_ref, m_ref):
    tm = MIX_TM
    sub = tm // MIX_ROW_SPLIT
    i_in_seq = pl.program_id(0) % tiles_per_seq

    ebuf_ref[0:POOL_HALO, :] = jnp.where(i_in_seq == 0, 0.0, uh_ref[...])
    ebuf_ref[POOL_HALO:, :] = u_ref[...]

    n_groups = len(POOL_WINDOWS)
    a_sub = tm // n_groups
    head_pos = i_in_seq * tm + lax.broadcasted_iota(jnp.int32, (POOL_HALO, POOL_GROUP), 0)
    for g, w in enumerate(POOL_WINDOWS):
        rows = slice(g * a_sub, (g + 1) * a_sub)
        m_ref[rows, :] = jnp.dot(a_ref[rows, :], woa_ref[...], preferred_element_type=F32)

        cs = slice(g * POOL_GROUP, (g + 1) * POOL_GROUP)
        tok = ebuf_ref[POOL_HALO:, cs]
        wsum = tok
        for k in range(1, w):
            wsum = wsum + ebuf_ref[POOL_HALO - k:POOL_HALO - k + tm, cs]
        body = wsum * (1.0 / w) - tok
        head_count = jnp.minimum(head_pos + 1, w).astype(F32)
        head = wsum[:POOL_HALO] / head_count - tok[:POOL_HALO]
        pooled = jnp.concatenate([head, body[POOL_HALO:]], axis=0).astype(BF16)
        mixed = jnp.dot(pooled, pw_ref[g], preferred_element_type=F32)
        p_ref[:, cs] = (mixed * ps_ref[:, cs]).astype(BF16)

    for r in range(MIX_ROW_SPLIT):
        rows = slice(r * sub, (r + 1) * sub)
        m = m_ref[rows, :] + jnp.dot(p_ref[rows, :], wop_ref[...], preferred_element_type=F32)
        o_ref[rows, :] = x_ref[rows, :] + (m * _rms_scale(m)) * g_ref[...]


def _mixout(layer, x, a, u, pool_w, pool_scale, w_o, g, seq_len):
    t = x.shape[0]
    tm = MIX_TM
    tiles_per_seq = seq_len // tm
    halo_blocks = tm // POOL_HALO
    n_groups = len(POOL_WINDOWS)
    return pl.pallas_call(
        functools.partial(_mixout_kernel, tiles_per_seq),
        grid=(t // tm,),
        in_specs=[
            pl.BlockSpec((tm, D_MODEL), lambda i: (i, 0)),
            pl.BlockSpec((tm, ATTN_WIDTH), lambda i: (i, 0)),
            pl.BlockSpec((tm, POOL_WIDTH), lambda i: (i, 0)),
            pl.BlockSpec((POOL_HALO, POOL_WIDTH), lambda i: (jnp.maximum(i * halo_blocks - 1, 0), 0)),
            pl.BlockSpec((None, n_groups, POOL_GROUP, POOL_GROUP), lambda i: (layer, 0, 0, 0)),
            pl.BlockSpec((None, 1, POOL_WIDTH), lambda i: (layer, 0, 0)),
            pl.BlockSpec((None, ATTN_WIDTH, D_MODEL), lambda i: (layer, 0, 0)),
            pl.BlockSpec((None, POOL_WIDTH, D_MODEL), lambda i: (layer, 1, 0)),
            pl.BlockSpec((None, 1, D_MODEL), lambda i: (layer, 0, 0)),
        ],
        out_specs=pl.BlockSpec((tm, D_MODEL), lambda i: (i, 0)),
        out_shape=jax.ShapeDtypeStruct((t, D_MODEL), F32),
        scratch_shapes=[pltpu.VMEM((tm + POOL_HALO, POOL_WIDTH), F32),
                        pltpu.VMEM((tm, POOL_WIDTH), BF16),
                        pltpu.VMEM((tm, D_MODEL), F32)],
        compiler_params=pltpu.CompilerParams(
            dimension_semantics=("arbitrary",),
            vmem_limit_bytes=VMEM_LIMIT_BYTES),
        name="mixout",
    )(x, a, u, u, pool_w, pool_scale, w_o, w_o, g)


def _gelu_tanh(x):
    return 0.5 * x * (1.0 + jnp.tanh(math.sqrt(2.0 / math.pi) * (x + 0.044715 * (x * x * x))))


def _ffn_kernel(tiles_per_seq, x_ref, g_ref, wg_ref, wv_ref, cwg_ref, cwv_ref, cbg_ref, cbv_ref,
                wd_ref, pg_ref, o_ref, h_ref, ug_ref, uv_ref, carry_g_ref, carry_v_ref):
    tm = FFN_TM
    i = pl.program_id(0)
    j = pl.program_id(1)
    first_in_seq = (i % tiles_per_seq) == 0
    y_ref = o_ref

    @pl.when(j == 0)
    def _():
        x = x_ref[...]
        h_ref[...] = ((x * _rms_scale(x)) * g_ref[...]).astype(BF16)

    @pl.when(jnp.logical_and(i == 0, j == 0))
    def _():
        y_ref[...] = jnp.zeros_like(y_ref)

    sub = tm // FFN_ROW_SPLIT

    def up(r, w_ref, buf_ref):
        rows = slice(r * sub, (r + 1) * sub)
        buf_ref[CONV_HALO + r * sub:CONV_HALO + (r + 1) * sub, :] = jnp.dot(
            h_ref[rows, :], w_ref[...], preferred_element_type=F32)

    def conv(r, cw_ref, cb_ref, buf_ref):
        out = cb_ref[...]
        for tap in range(CONV_WIDTH):
            start = CONV_HALO - (CONV_WIDTH - 1) + tap + r * sub
            out = out + buf_ref[start:start + sub, :] * cw_ref[tap:tap + 1, :]
        return out

    ug_ref[0:CONV_HALO, :] = jnp.where(first_in_seq, 0.0, carry_g_ref[j])
    uv_ref[0:CONV_HALO, :] = jnp.where(first_in_seq, 0.0, carry_v_ref[j])
    for r in range(FFN_ROW_SPLIT):
        up(r, wg_ref, ug_ref)
        up(r, wv_ref, uv_ref)
    carry_g_ref[j] = ug_ref[tm:tm + CONV_HALO, :]
    carry_v_ref[j] = uv_ref[tm:tm + CONV_HALO, :]
    for r in range(FFN_ROW_SPLIT):
        rows = slice(r * sub, (r + 1) * sub)
        gate = conv(r, cwg_ref, cbg_ref, ug_ref)
        val = conv(r, cwv_ref, cbv_ref, uv_ref)
        act = (_gelu_tanh(gate) * val).astype(BF16)
        acc = jnp.where(j == 0, 0.0, y_ref[rows, :])
        y_ref[rows, :] = acc + jnp.dot(act, wd_ref[...], preferred_element_type=F32)

    @pl.when(j == pl.num_programs(1) - 1)
    def _():
        for c in range(tm // NORM_ROWS):
            rows = slice(c * NORM_ROWS, (c + 1) * NORM_ROWS)
            yy = y_ref[rows, :]
            o_ref[rows, :] = x_ref[rows, :] + (yy * _rms_scale(yy)) * pg_ref[...]


def _ffn(layer, x, g, w_up, conv_w, conv_b, w_down, post_g, seq_len):
    t = x.shape[0]
    tm, tf = FFN_TM, FFN_TF
    n_f = D_FF // tf
    tiles_per_seq = seq_len // tm
    return pl.pallas_call(
        functools.partial(_ffn_kernel, tiles_per_seq),
        grid=(t // tm, n_f),
        in_specs=[
            pl.BlockSpec((tm, D_MODEL), lambda i, j: (i, 0)),
            pl.BlockSpec((None, 1, D_MODEL), lambda i, j: (layer, 0, 0)),
            pl.BlockSpec((None, D_MODEL, tf), lambda i, j: (layer, 0, j)),
            pl.BlockSpec((None, D_MODEL, tf), lambda i, j: (layer, 0, j + n_f)),
            pl.BlockSpec((None, CONV_WIDTH, tf), lambda i, j: (layer, 0, j)),
            pl.BlockSpec((None, CONV_WIDTH, tf), lambda i, j: (layer, 0, j + n_f)),
            pl.BlockSpec((None, 1, tf), lambda i, j: (layer, 0, j)),
            pl.BlockSpec((None, 1, tf), lambda i, j: (layer, 0, j + n_f)),
            pl.BlockSpec((None, tf, D_MODEL), lambda i, j: (layer, j, 0)),
            pl.BlockSpec((None, 1, D_MODEL), lambda i, j: (layer, 0, 0)),
        ],
        out_specs=pl.BlockSpec((tm, D_MODEL), lambda i, j: (i, 0)),
        out_shape=jax.ShapeDtypeStruct((t, D_MODEL), F32),
        scratch_shapes=[
            pltpu.VMEM((tm, D_MODEL), BF16),
            pltpu.VMEM((tm + CONV_HALO, tf), F32),
            pltpu.VMEM((tm + CONV_HALO, tf), F32),
            pltpu.VMEM((n_f, CONV_HALO, tf), F32),
            pltpu.VMEM((n_f, CONV_HALO, tf), F32),
        ],
        compiler_params=pltpu.CompilerParams(
            dimension_semantics=("arbitrary", "arbitrary"),
            vmem_limit_bytes=VMEM_LIMIT_BYTES),
        name="ffn",
    )(x, g, w_up, w_up, conv_w, conv_w, conv_b, conv_b, w_down, post_g)


def kernel(x, pre_mix_g, w_in, rel_bias, pool_w, pool_scale, w_o, post_mix_g,
           pre_ffn_g, w_up, conv_w, conv_b, w_down, post_ffn_g):
    b, s, d = x.shape
    depth = w_in.shape[0]
    w_in = jnp.concatenate([w_in[..., ATTN_WIDTH:3 * ATTN_WIDTH], w_in[..., :ATTN_WIDTH],
                            w_in[..., 3 * ATTN_WIDTH:]], axis=-1)
    w_in, pool_w, w_o, w_up, w_down = (w.astype(BF16) for w in (w_in, pool_w, w_o, w_up, w_down))
    pre_mix_g, pool_scale, post_mix_g, pre_ffn_g, conv_b, post_ffn_g = (
        _row(v) for v in (pre_mix_g, pool_scale, post_mix_g, pre_ffn_g, conv_b, post_ffn_g))
    xt = x.reshape(b * s, d)
    for l in range(depth):
        z, u = _inproj(l, xt, pre_mix_g, w_in)
        a = _attention(z, _band_bias(rel_bias[l]), s)
        xt = _mixout(l, xt, a, u, pool_w, pool_scale, w_o, post_mix_g, s)
        xt = _ffn(l, xt, pre_ffn_g, w_up, conv_w, conv_b, w_down, post_ffn_g, s)
    return xt.reshape(b, s, d)
```

```python
import functools
import math

import jax
import jax.numpy as jnp
from jax import lax
from jax.experimental import pallas as pl
from jax.experimental.pallas import tpu as pltpu

F32 = jnp.float32
BF16 = jnp.bfloat16

D_MODEL = 2048
CHUNK = 64
LEFT_CHUNKS = 8
ATTN_WIDTH = 1024
POOL_WIDTH = 1024
HEAD_DIM = 128
N_HEADS = ATTN_WIDTH // HEAD_DIM
REL_CLIP = 128
POOL_WINDOWS = (2, 4, 8, 16)
POOL_GROUP = POOL_WIDTH // len(POOL_WINDOWS)
IN_WIDTH = 3 * ATTN_WIDTH + POOL_WIDTH
D_FF = 5632
CONV_WIDTH = 3
NORM_EPS = 1e-6
MASK_VALUE = -1e30
LOG2_E = math.log2(math.e)
QK_SCALE = LOG2_E / math.sqrt(HEAD_DIM)
Q_COL, K_COL, V_COL = 0, ATTN_WIDTH, 2 * ATTN_WIDTH

V7X_VMEM_BYTES = 64 * 1024 * 1024
VMEM_LIMIT_BYTES = V7X_VMEM_BYTES - 6 * 1024 * 1024

IN_TM = 1024
IN_TN = 2048
ATT_BQ = 4 * CHUNK
ATT_NKB = LEFT_CHUNKS * CHUNK // ATT_BQ + 1
ATT_BK = ATT_NKB * ATT_BQ
MIX_TM = 512
MIX_ROW_SPLIT = 2
POOL_HALO = 16
FFN_TM = 512
FFN_TF = 512
FFN_ROW_SPLIT = 2
CONV_HALO = 8
NORM_ROWS = 16


def _rms_scale(x):
    return lax.rsqrt(jnp.mean(x * x, axis=-1, keepdims=True) + NORM_EPS)


def _row(vec):
    return vec.reshape(vec.shape[0], 1, vec.shape[1])


def _inproj_kernel(x_ref, g_ref, w_ref, z_ref, u_ref, h_ref):
    j = pl.program_id(1)

    @pl.when(j == 0)
    def _():
        x = x_ref[...]
        h_ref[...] = ((x * _rms_scale(x)) * g_ref[...]).astype(BF16)

    z = jnp.dot(h_ref[...], w_ref[...], preferred_element_type=F32)
    q_scale = jnp.where(j == 0, QK_SCALE, 1.0).astype(F32)
    z_ref[:, :ATTN_WIDTH] = (z[:, :ATTN_WIDTH] * q_scale).astype(BF16)
    z_ref[:, ATTN_WIDTH:] = z[:, ATTN_WIDTH:].astype(BF16)
    u_ref[...] = z[:, IN_TN - POOL_WIDTH:]


def _inproj(layer, x, g, w):
    t = x.shape[0]
    grid = (t // IN_TM, IN_WIDTH // IN_TN)
    return pl.pallas_call(
        _inproj_kernel,
        grid=grid,
        in_specs=[
            pl.BlockSpec((IN_TM, D_MODEL), lambda i, j: (i, 0)),
            pl.BlockSpec((None, 1, D_MODEL), lambda i, j: (layer, 0, 0)),
            pl.BlockSpec((None, D_MODEL, IN_TN), lambda i, j: (layer, 0, j)),
        ],
        out_specs=[
            pl.BlockSpec((IN_TM, IN_TN), lambda i, j: (i, j)),
            pl.BlockSpec((IN_TM, POOL_WIDTH), lambda i, j: (i, 0)),
        ],
        out_shape=[
            jax.ShapeDtypeStruct((t, IN_WIDTH), BF16),
            jax.ShapeDtypeStruct((t, POOL_WIDTH), F32),
        ],
        scratch_shapes=[pltpu.VMEM((IN_TM, D_MODEL), BF16)],
        compiler_params=pltpu.CompilerParams(
            dimension_semantics=("arbitrary", "arbitrary"),
            vmem_limit_bytes=VMEM_LIMIT_BYTES),
        name="inproj",
    )(x, g, w)


def _band_bias(table):
    n_heads = table.shape[0]
    band_keys = (LEFT_CHUNKS + 1) * CHUNK
    far = band_keys - REL_CLIP
    near = REL_CLIP + CHUNK - 1
    table = table.astype(F32)
    ext = jnp.concatenate(
        [jnp.broadcast_to(table[:, 2 * REL_CLIP:], (n_heads, far)),
         table[:, 2 * REL_CLIP - near:2 * REL_CLIP][:, ::-1]], axis=1)
    rows = jnp.stack([ext[:, CHUNK - 1 - qi:CHUNK - 1 - qi + band_keys] for qi in range(CHUNK)], axis=1)
    rows = rows * LOG2_E
    pad = ATT_BK - band_keys
    masked = jnp.full((n_heads, CHUNK, pad), MASK_VALUE, F32)
    blocks = [jnp.concatenate([masked[:, :, :c * CHUNK], rows, masked[:, :, :pad - c * CHUNK]], axis=2)
              for c in range(ATT_BQ // CHUNK)]
    bias = jnp.concatenate(blocks, axis=1)
    key_block = lax.broadcasted_iota(jnp.int32, (ATT_NKB, 1, 1, ATT_BK), 3) // ATT_BQ
    n_masked = lax.broadcasted_iota(jnp.int32, (ATT_NKB, 1, 1, ATT_BK), 0)
    return jnp.where(key_block < n_masked, F32(MASK_VALUE), bias[None])


def _attn_kernel(kvq_ref, bias_ref, o_ref, kv_ref, p_ref, linv_ref):
    @pl.when(pl.program_id(0) == 0)
    def _():
        kv_ref[...] = jnp.zeros_like(kv_ref)
        p_ref[...] = jnp.zeros_like(p_ref)
        linv_ref[...] = jnp.zeros_like(linv_ref)

    kv_ref[0:ATT_BK, :] = kv_ref[ATT_BQ:ATT_BK + ATT_BQ, :]
    kv_ref[ATT_BK:ATT_BK + ATT_BQ, :] = kvq_ref[:, K_COL:K_COL + 2 * ATTN_WIDTH]
    k_rows = slice(ATT_BQ, ATT_BK + ATT_BQ)
    v_rows = slice(0, ATT_BK)

    for h in range(N_HEADS):
        vs = slice(ATTN_WIDTH + h * HEAD_DIM, ATTN_WIDTH + (h + 1) * HEAD_DIM)
        o = jnp.dot(p_ref[h], kv_ref[v_rows, vs], preferred_element_type=F32)
        o_ref[:, h * HEAD_DIM:(h + 1) * HEAD_DIM] = (o * linv_ref[h]).astype(BF16)

    for h in range(N_HEADS):
        hs = slice(h * HEAD_DIM, (h + 1) * HEAD_DIM)
        qs = slice(Q_COL + h * HEAD_DIM, Q_COL + (h + 1) * HEAD_DIM)
        s = lax.dot_general(kvq_ref[:, qs], kv_ref[k_rows, hs], (((1,), (1,)), ((), ())),
                            preferred_element_type=F32)
        s = s + bias_ref[h]
        m = jnp.max(s, axis=-1, keepdims=True)
        p = jnp.exp2(s - m)
        l = jnp.sum(p, axis=-1, keepdims=True)
        p_ref[h] = p.astype(BF16)
        linv_ref[h] = jnp.broadcast_to(1.0 / l, (ATT_BQ, HEAD_DIM))


def _attention(z, bias, seq_len):
    t = z.shape[0]
    n_blocks = t // ATT_BQ
    blocks_per_seq = seq_len // ATT_BQ

    def cur_block(i):
        return jnp.minimum(i, n_blocks - 1)

    def bias_map(i):
        return (jnp.maximum(ATT_NKB - 1 - cur_block(i) % blocks_per_seq, 0), 0, 0, 0)

    return pl.pallas_call(
        _attn_kernel,
        grid=(n_blocks + 1,),
        in_specs=[pl.BlockSpec((ATT_BQ, 3 * ATTN_WIDTH), lambda i: (cur_block(i), 0)),
                  pl.BlockSpec((None, N_HEADS, ATT_BQ, ATT_BK), bias_map)],
        out_specs=pl.BlockSpec((ATT_BQ, ATTN_WIDTH), lambda i: (jnp.maximum(i - 1, 0), 0)),
        out_shape=jax.ShapeDtypeStruct((t, ATTN_WIDTH), BF16),
        scratch_shapes=[pltpu.VMEM((ATT_BK + ATT_BQ, 2 * ATTN_WIDTH), BF16),
                        pltpu.VMEM((N_HEADS, ATT_BQ, ATT_BK), BF16),
                        pltpu.VMEM((N_HEADS, ATT_BQ, HEAD_DIM), F32)],
        compiler_params=pltpu.CompilerParams(
            dimension_semantics=("arbitrary",),
            vmem_limit_bytes=VMEM_LIMIT_BYTES),
        name="attention",
    )(z, bias)


def _mixout_kernel(tiles_per_seq, x_ref, a_ref, u_ref, uh_ref, pw_ref, ps_ref, woa_ref, wop_ref,
                   g_ref, o_ref, ebuf_ref, p_ref, m_ref):
    tm = MIX_TM
    sub = tm // MIX_ROW_SPLIT
    i_in_seq = pl.program_id(0) % tiles_per_seq

    ebuf_ref[0:POOL_HALO, :] = jnp.where(i_in_seq == 0, 0.0, uh_ref[...])
    ebuf_ref[POOL_HALO:, :] = u_ref[...]

    n_groups = len(POOL_WINDOWS)
    a_sub = tm // n_groups
    head_pos = i_in_seq * tm + lax.broadcasted_iota(jnp.int32, (POOL_HALO, POOL_GROUP), 0)

    def attn_rows(r):
        rows = slice(r * a_sub, (r + 1) * a_sub)
        m_ref[rows, :] = jnp.dot(a_ref[rows, :], woa_ref[...], preferred_element_type=F32)

    for g, w in enumerate(POOL_WINDOWS):
        if g > 0:
            attn_rows(g - 1)
        if g == n_groups - 1:
            attn_rows(g)

        cs = slice(g * POOL_GROUP, (g + 1) * POOL_GROUP)
        tok = ebuf_ref[POOL_HALO:, cs]
        wsum = tok
        for k in range(1, w):
            wsum = wsum + ebuf_ref[POOL_HALO - k:POOL_HALO - k + tm, cs]
        body = wsum * (1.0 / w) - tok
        head_count = jnp.minimum(head_pos + 1, w).astype(F32)
        head = wsum[:POOL_HALO] / head_count - tok[:POOL_HALO]
        pooled = jnp.concatenate([head, body[POOL_HALO:]], axis=0).astype(BF16)
        mixed = jnp.dot(pooled, pw_ref[g], preferred_element_type=F32)
        p_ref[:, cs] = (mixed * ps_ref[:, cs]).astype(BF16)

    for r in range(MIX_ROW_SPLIT):
        rows = slice(r * sub, (r + 1) * sub)
        m = m_ref[rows, :] + jnp.dot(p_ref[rows, :], wop_ref[...], preferred_element_type=F32)
        o_ref[rows, :] = x_ref[rows, :] + (m * _rms_scale(m)) * g_ref[...]


def _mixout(layer, x, a, u, pool_w, pool_scale, w_o, g, seq_len):
    t = x.shape[0]
    tm = MIX_TM
    tiles_per_seq = seq_len // tm
    halo_blocks = tm // POOL_HALO
    n_groups = len(POOL_WINDOWS)
    return pl.pallas_call(
        functools.partial(_mixout_kernel, tiles_per_seq),
        grid=(t // tm,),
        in_specs=[
            pl.BlockSpec((tm, D_MODEL), lambda i: (i, 0)),
            pl.BlockSpec((tm, ATTN_WIDTH), lambda i: (i, 0)),
            pl.BlockSpec((tm, POOL_WIDTH), lambda i: (i, 0)),
            pl.BlockSpec((POOL_HALO, POOL_WIDTH), lambda i: (jnp.maximum(i * halo_blocks - 1, 0), 0)),
            pl.BlockSpec((None, n_groups, POOL_GROUP, POOL_GROUP), lambda i: (layer, 0, 0, 0)),
            pl.BlockSpec((None, 1, POOL_WIDTH), lambda i: (layer, 0, 0)),
            pl.BlockSpec((None, ATTN_WIDTH, D_MODEL), lambda i: (layer, 0, 0)),
            pl.BlockSpec((None, POOL_WIDTH, D_MODEL), lambda i: (layer, 1, 0)),
            pl.BlockSpec((None, 1, D_MODEL), lambda i: (layer, 0, 0)),
        ],
        out_specs=pl.BlockSpec((tm, D_MODEL), lambda i: (i, 0)),
        out_shape=jax.ShapeDtypeStruct((t, D_MODEL), F32),
        scratch_shapes=[pltpu.VMEM((tm + POOL_HALO, POOL_WIDTH), F32),
                        pltpu.VMEM((tm, POOL_WIDTH), BF16),
                        pltpu.VMEM((tm, D_MODEL), F32)],
        compiler_params=pltpu.CompilerParams(
            dimension_semantics=("arbitrary",),
            vmem_limit_bytes=VMEM_LIMIT_BYTES),
        name="mixout",
    )(x, a, u, u, pool_w, pool_scale, w_o, w_o, g)


def _gelu_tanh(x):
    return 0.5 * x * (1.0 + jnp.tanh(math.sqrt(2.0 / math.pi) * (x + 0.044715 * (x * x * x))))


def _ffn_kernel(tiles_per_seq, x_ref, g_ref, wg_ref, wv_ref, cwg_ref, cwv_ref, cbg_ref, cbv_ref,
                wd_ref, pg_ref, o_ref, h_ref, ug_ref, uv_ref, carry_g_ref, carry_v_ref):
    tm = FFN_TM
    i = pl.program_id(0)
    j = pl.program_id(1)
    first_in_seq = (i % tiles_per_seq) == 0
    y_ref = o_ref

    @pl.when(j == 0)
    def _():
        x = x_ref[...]
        h_ref[...] = ((x * _rms_scale(x)) * g_ref[...]).astype(BF16)

    @pl.when(jnp.logical_and(i == 0, j == 0))
    def _():
        y_ref[...] = jnp.zeros_like(y_ref)

    sub = tm // FFN_ROW_SPLIT

    def up(r, w_ref, buf_ref):
        rows = slice(r * sub, (r + 1) * sub)
        buf_ref[CONV_HALO + r * sub:CONV_HALO + (r + 1) * sub, :] = jnp.dot(
            h_ref[rows, :], w_ref[...], preferred_element_type=F32)

    def conv(r, cw_ref, cb_ref, buf_ref):
        out = cb_ref[...]
        for tap in range(CONV_WIDTH):
            start = CONV_HALO - (CONV_WIDTH - 1) + tap + r * sub
            out = out + buf_ref[start:start + sub, :] * cw_ref[tap:tap + 1, :]
        return out

    ug_ref[0:CONV_HALO, :] = jnp.where(first_in_seq, 0.0, carry_g_ref[j])
    uv_ref[0:CONV_HALO, :] = jnp.where(first_in_seq, 0.0, carry_v_ref[j])
    for r in range(FFN_ROW_SPLIT):
        up(r, wg_ref, ug_ref)
        up(r, wv_ref, uv_ref)
    carry_g_ref[j] = ug_ref[tm:tm + CONV_HALO, :]
    carry_v_ref[j] = uv_ref[tm:tm + CONV_HALO, :]
    for r in range(FFN_ROW_SPLIT):
        rows = slice(r * sub, (r + 1) * sub)
        gate = conv(r, cwg_ref, cbg_ref, ug_ref)
        val = conv(r, cwv_ref, cbv_ref, uv_ref)
        act = (_gelu_tanh(gate) * val).astype(BF16)
        acc = jnp.where(j == 0, 0.0, y_ref[rows, :])
        y_ref[rows, :] = acc + jnp.dot(act, wd_ref[...], preferred_element_type=F32)

    @pl.when(j == pl.num_programs(1) - 1)
    def _():
        for c in range(tm // NORM_ROWS):
            rows = slice(c * NORM_ROWS, (c + 1) * NORM_ROWS)
            yy = y_ref[rows, :]
            o_ref[rows, :] = x_ref[rows, :] + (yy * _rms_scale(yy)) * pg_ref[...]


def _ffn(layer, x, g, w_up, conv_w, conv_b, w_down, post_g, seq_len):
    t = x.shape[0]
    tm, tf = FFN_TM, FFN_TF
    n_f = D_FF // tf
    tiles_per_seq = seq_len // tm
    return pl.pallas_call(
        functools.partial(_ffn_kernel, tiles_per_seq),
        grid=(t // tm, n_f),
        in_specs=[
            pl.BlockSpec((tm, D_MODEL), lambda i, j: (i, 0)),
            pl.BlockSpec((None, 1, D_MODEL), lambda i, j: (layer, 0, 0)),
            pl.BlockSpec((None, D_MODEL, tf), lambda i, j: (layer, 0, j)),
            pl.BlockSpec((None, D_MODEL, tf), lambda i, j: (layer, 0, j + n_f)),
            pl.BlockSpec((None, CONV_WIDTH, tf), lambda i, j: (layer, 0, j)),
            pl.BlockSpec((None, CONV_WIDTH, tf), lambda i, j: (layer, 0, j + n_f)),
            pl.BlockSpec((None, 1, tf), lambda i, j: (layer, 0, j)),
            pl.BlockSpec((None, 1, tf), lambda i, j: (layer, 0, j + n_f)),
            pl.BlockSpec((None, tf, D_MODEL), lambda i, j: (layer, j, 0)),
            pl.BlockSpec((None, 1, D_MODEL), lambda i, j: (layer, 0, 0)),
        ],
        out_specs=pl.BlockSpec((tm, D_MODEL), lambda i, j: (i, 0)),
        out_shape=jax.ShapeDtypeStruct((t, D_MODEL), F32),
        scratch_shapes=[
            pltpu.VMEM((tm, D_MODEL), BF16),
            pltpu.VMEM((tm + CONV_HALO, tf), F32),
            pltpu.VMEM((tm + CONV_HALO, tf), F32),
            pltpu.VMEM((n_f, CONV_HALO, tf), F32),
            pltpu.VMEM((n_f, CONV_HALO, tf), F32),
        ],
        compiler_params=pltpu.CompilerParams(
            dimension_semantics=("arbitrary", "arbitrary"),
            vmem_limit_bytes=VMEM_LIMIT_BYTES),
        name="ffn",
    )(x, g, w_up, w_up, conv_w, conv_w, conv_b, conv_b, w_down, post_g)


def kernel(x, pre_mix_g, w_in, rel_bias, pool_w, pool_scale, w_o, post_mix_g,
           pre_ffn_g, w_up, conv_w, conv_b, w_down, post_ffn_g):
    b, s, d = x.shape
    depth = w_in.shape[0]
    w_in, pool_w, w_o, w_up, w_down = (w.astype(BF16) for w in (w_in, pool_w, w_o, w_up, w_down))
    pre_mix_g, pool_scale, post_mix_g, pre_ffn_g, conv_b, post_ffn_g = (
        _row(v) for v in (pre_mix_g, pool_scale, post_mix_g, pre_ffn_g, conv_b, post_ffn_g))
    xt = x.reshape(b * s, d)
    for l in range(depth):
        z, u = _inproj(l, xt, pre_mix_g, w_in)
        a = _attention(z, _band_bias(rel_bias[l]), s)
        xt = _mixout(l, xt, a, u, pool_w, pool_scale, w_o, post_mix_g, s)
        xt = _ffn(l, xt, pre_ffn_g, w_up, conv_w, conv_b, w_down, post_ffn_g, s)
    return xt.reshape(b, s, d)
```

```python
import functools
import math

import jax
import jax.numpy as jnp
from jax import lax
from jax.experimental import pallas as pl
from jax.experimental.pallas import tpu as pltpu

F32 = jnp.float32
BF16 = jnp.bfloat16

D_MODEL = 2048
CHUNK = 64
LEFT_CHUNKS = 8
ATTN_WIDTH = 1024
POOL_WIDTH = 1024
HEAD_DIM = 128
N_HEADS = ATTN_WIDTH // HEAD_DIM
REL_CLIP = 128
POOL_WINDOWS = (2, 4, 8, 16)
POOL_GROUP = POOL_WIDTH // len(POOL_WINDOWS)
IN_WIDTH = 3 * ATTN_WIDTH + POOL_WIDTH
D_FF = 5632
CONV_WIDTH = 3
NORM_EPS = 1e-6
MASK_VALUE = -1e30
LOG2_E = math.log2(math.e)
QK_SCALE = LOG2_E / math.sqrt(HEAD_DIM)
Q_COL, K_COL, V_COL = 0, ATTN_WIDTH, 2 * ATTN_WIDTH

V7X_VMEM_BYTES = 64 * 1024 * 1024
VMEM_LIMIT_BYTES = V7X_VMEM_BYTES - 1 * 1024 * 1024

IN_TM = 1024
IN_TN = 2048
ATT_BQ = 4 * CHUNK
ATT_NKB = LEFT_CHUNKS * CHUNK // ATT_BQ + 1
ATT_BK = ATT_NKB * ATT_BQ
MIX_TM = 512
MIX_ROW_SPLIT = 2
POOL_HALO = 16
FFN_TM = 1024
FFN_TF = 512
FFN_ROW_SPLIT = 2
CONV_HALO = 8
NORM_ROWS = 16


def _rms_scale(x):
    return lax.rsqrt(jnp.mean(x * x, axis=-1, keepdims=True) + NORM_EPS)


def _row(vec):
    return vec.reshape(vec.shape[0], 1, vec.shape[1])


def _inproj_kernel(x_ref, g_ref, w_ref, z_ref, u_ref, h_ref):
    j = pl.program_id(1)

    @pl.when(j == 0)
    def _():
        x = x_ref[...]
        h_ref[...] = ((x * _rms_scale(x)) * g_ref[...]).astype(BF16)

    z = jnp.dot(h_ref[...], w_ref[...], preferred_element_type=F32)
    q_scale = jnp.where(j == 0, QK_SCALE, 1.0).astype(F32)
    z_ref[:, :ATTN_WIDTH] = (z[:, :ATTN_WIDTH] * q_scale).astype(BF16)
    z_ref[:, ATTN_WIDTH:] = z[:, ATTN_WIDTH:].astype(BF16)
    u_ref[...] = z[:, IN_TN - POOL_WIDTH:]


def _inproj(layer, x, g, w):
    t = x.shape[0]
    grid = (t // IN_TM, IN_WIDTH // IN_TN)
    return pl.pallas_call(
        _inproj_kernel,
        grid=grid,
        in_specs=[
            pl.BlockSpec((IN_TM, D_MODEL), lambda i, j: (i, 0)),
            pl.BlockSpec((None, 1, D_MODEL), lambda i, j: (layer, 0, 0)),
            pl.BlockSpec((None, D_MODEL, IN_TN), lambda i, j: (layer, 0, j)),
        ],
        out_specs=[
            pl.BlockSpec((IN_TM, IN_TN), lambda i, j: (i, j)),
            pl.BlockSpec((IN_TM, POOL_WIDTH), lambda i, j: (i, 0)),
        ],
        out_shape=[
            jax.ShapeDtypeStruct((t, IN_WIDTH), BF16),
            jax.ShapeDtypeStruct((t, POOL_WIDTH), F32),
        ],
        scratch_shapes=[pltpu.VMEM((IN_TM, D_MODEL), BF16)],
        compiler_params=pltpu.CompilerParams(
            dimension_semantics=("arbitrary", "arbitrary"),
            vmem_limit_bytes=VMEM_LIMIT_BYTES),
        name="inproj",
    )(x, g, w)


def _band_bias(table):
    n_heads = table.shape[0]
    band_keys = (LEFT_CHUNKS + 1) * CHUNK
    far = band_keys - REL_CLIP
    near = REL_CLIP + CHUNK - 1
    table = table.astype(F32)
    ext = jnp.concatenate(
        [jnp.broadcast_to(table[:, 2 * REL_CLIP:], (n_heads, far)),
         table[:, 2 * REL_CLIP - near:2 * REL_CLIP][:, ::-1]], axis=1)
    rows = jnp.stack([ext[:, CHUNK - 1 - qi:CHUNK - 1 - qi + band_keys] for qi in range(CHUNK)], axis=1)
    rows = rows * LOG2_E
    pad = ATT_BK - band_keys
    masked = jnp.full((n_heads, CHUNK, pad), MASK_VALUE, F32)
    blocks = [jnp.concatenate([masked[:, :, :c * CHUNK], rows, masked[:, :, :pad - c * CHUNK]], axis=2)
              for c in range(ATT_BQ // CHUNK)]
    bias = jnp.concatenate(blocks, axis=1)
    key_block = lax.broadcasted_iota(jnp.int32, (ATT_NKB, 1, 1, ATT_BK), 3) // ATT_BQ
    n_masked = lax.broadcasted_iota(jnp.int32, (ATT_NKB, 1, 1, ATT_BK), 0)
    return jnp.where(key_block < n_masked, F32(MASK_VALUE), bias[None])


def _attn_kernel(kvq_ref, bias_ref, o_ref, kv_ref, p_ref, linv_ref):
    @pl.when(pl.program_id(0) == 0)
    def _():
        kv_ref[...] = jnp.zeros_like(kv_ref)
        p_ref[...] = jnp.zeros_like(p_ref)
        linv_ref[...] = jnp.zeros_like(linv_ref)

    kv_ref[0:ATT_BK, :] = kv_ref[ATT_BQ:ATT_BK + ATT_BQ, :]
    kv_ref[ATT_BK:ATT_BK + ATT_BQ, :] = kvq_ref[:, K_COL:K_COL + 2 * ATTN_WIDTH]
    k_rows = slice(ATT_BQ, ATT_BK + ATT_BQ)
    v_rows = slice(0, ATT_BK)

    for h in range(N_HEADS):
        vs = slice(ATTN_WIDTH + h * HEAD_DIM, ATTN_WIDTH + (h + 1) * HEAD_DIM)
        o = jnp.dot(p_ref[h], kv_ref[v_rows, vs], preferred_element_type=F32)
        o_ref[:, h * HEAD_DIM:(h + 1) * HEAD_DIM] = (o * linv_ref[h]).astype(BF16)

    for h in range(N_HEADS):
        hs = slice(h * HEAD_DIM, (h + 1) * HEAD_DIM)
        qs = slice(Q_COL + h * HEAD_DIM, Q_COL + (h + 1) * HEAD_DIM)
        s = lax.dot_general(kvq_ref[:, qs], kv_ref[k_rows, hs], (((1,), (1,)), ((), ())),
                            preferred_element_type=F32)
        s = s + bias_ref[h]
        m = jnp.max(s, axis=-1, keepdims=True)
        p = jnp.exp2(s - m)
        l = jnp.sum(p, axis=-1, keepdims=True)
        p_ref[h] = p.astype(BF16)
        linv_ref[h] = jnp.broadcast_to(1.0 / l, (ATT_BQ, HEAD_DIM))


def _attention(z, bias, seq_len):
    t = z.shape[0]
    n_blocks = t // ATT_BQ
    blocks_per_seq = seq_len // ATT_BQ

    def cur_block(i):
        return jnp.minimum(i, n_blocks - 1)

    def bias_map(i):
        return (jnp.maximum(ATT_NKB - 1 - cur_block(i) % blocks_per_seq, 0), 0, 0, 0)

    return pl.pallas_call(
        _attn_kernel,
        grid=(n_blocks + 1,),
        in_specs=[pl.BlockSpec((ATT_BQ, 3 * ATTN_WIDTH), lambda i: (cur_block(i), 0)),
                  pl.BlockSpec((None, N_HEADS, ATT_BQ, ATT_BK), bias_map)],
        out_specs=pl.BlockSpec((ATT_BQ, ATTN_WIDTH), lambda i: (jnp.maximum(i - 1, 0), 0)),
        out_shape=jax.ShapeDtypeStruct((t, ATTN_WIDTH), BF16),
        scratch_shapes=[pltpu.VMEM((ATT_BK + ATT_BQ, 2 * ATTN_WIDTH), BF16),
                        pltpu.VMEM((N_HEADS, ATT_BQ, ATT_BK), BF16),
                        pltpu.VMEM((N_HEADS, ATT_BQ, HEAD_DIM), F32)],
        compiler_params=pltpu.CompilerParams(
            dimension_semantics=("arbitrary",),
            vmem_limit_bytes=VMEM_LIMIT_BYTES),
        name="attention",
    )(z, bias)


def _mixout_kernel(tiles_per_seq, x_ref, a_ref, u_ref, uh_ref, pw_ref, ps_ref, woa_ref, wop_ref,
                   g_ref, o_ref, ebuf_ref, p_ref, m_ref):
    tm = MIX_TM
    sub = tm // MIX_ROW_SPLIT
    i_in_seq = pl.program_id(0) % tiles_per_seq

    ebuf_ref[0:POOL_HALO, :] = jnp.where(i_in_seq == 0, 0.0, uh_ref[...])
    ebuf_ref[POOL_HALO:, :] = u_ref[...]

    n_groups = len(POOL_WINDOWS)
    a_sub = tm // n_groups
    head_pos = i_in_seq * tm + lax.broadcasted_iota(jnp.int32, (POOL_HALO, POOL_GROUP), 0)

    def attn_rows(r):
        rows = slice(r * a_sub, (r + 1) * a_sub)
        m_ref[rows, :] = jnp.dot(a_ref[rows, :], woa_ref[...], preferred_element_type=F32)

    for g, w in enumerate(POOL_WINDOWS):
        if g > 0:
            attn_rows(g - 1)
        if g == n_groups - 1:
            attn_rows(g)

        cs = slice(g * POOL_GROUP, (g + 1) * POOL_GROUP)
        tok = ebuf_ref[POOL_HALO:, cs]
        wsum = tok
        for k in range(1, w):
            wsum = wsum + ebuf_ref[POOL_HALO - k:POOL_HALO - k + tm, cs]
        body = wsum * (1.0 / w) - tok
        head_count = jnp.minimum(head_pos + 1, w).astype(F32)
        head = wsum[:POOL_HALO] / head_count - tok[:POOL_HALO]
        pooled = jnp.concatenate([head, body[POOL_HALO:]], axis=0).astype(BF16)
        mixed = jnp.dot(pooled, pw_ref[g], preferred_element_type=F32)
        p_ref[:, cs] = (mixed * ps_ref[:, cs]).astype(BF16)

    for r in range(MIX_ROW_SPLIT):
        rows = slice(r * sub, (r + 1) * sub)
        m = m_ref[rows, :] + jnp.dot(p_ref[rows, :], wop_ref[...], preferred_element_type=F32)
        o_ref[rows, :] = x_ref[rows, :] + (m * _rms_scale(m)) * g_ref[...]


def _mixout(layer, x, a, u, pool_w, pool_scale, w_o, g, seq_len):
    t = x.shape[0]
    tm = MIX_TM
    tiles_per_seq = seq_len // tm
    halo_blocks = tm // POOL_HALO
    n_groups = len(POOL_WINDOWS)
    return pl.pallas_call(
        functools.partial(_mixout_kernel, tiles_per_seq),
        grid=(t // tm,),
        in_specs=[
            pl.BlockSpec((tm, D_MODEL), lambda i: (i, 0)),
            pl.BlockSpec((tm, ATTN_WIDTH), lambda i: (i, 0)),
            pl.BlockSpec((tm, POOL_WIDTH), lambda i: (i, 0)),
            pl.BlockSpec((POOL_HALO, POOL_WIDTH), lambda i: (jnp.maximum(i * halo_blocks - 1, 0), 0)),
            pl.BlockSpec((None, n_groups, POOL_GROUP, POOL_GROUP), lambda i: (layer, 0, 0, 0)),
            pl.BlockSpec((None, 1, POOL_WIDTH), lambda i: (layer, 0, 0)),
            pl.BlockSpec((None, ATTN_WIDTH, D_MODEL), lambda i: (layer, 0, 0)),
            pl.BlockSpec((None, POOL_WIDTH, D_MODEL), lambda i: (layer, 1, 0)),
            pl.BlockSpec((None, 1, D_MODEL), lambda i: (layer, 0, 0)),
        ],
        out_specs=pl.BlockSpec((tm, D_MODEL), lambda i: (i, 0)),
        out_shape=jax.ShapeDtypeStruct((t, D_MODEL), F32),
        scratch_shapes=[pltpu.VMEM((tm + POOL_HALO, POOL_WIDTH), F32),
                        pltpu.VMEM((tm, POOL_WIDTH), BF16),
                        pltpu.VMEM((tm, D_MODEL), F32)],
        compiler_params=pltpu.CompilerParams(
            dimension_semantics=("arbitrary",),
            vmem_limit_bytes=VMEM_LIMIT_BYTES),
        name="mixout",
    )(x, a, u, u, pool_w, pool_scale, w_o, w_o, g)


def _gelu_tanh(x):
    return 0.5 * x * (1.0 + jnp.tanh(math.sqrt(2.0 / math.pi) * (x + 0.044715 * (x * x * x))))


def _ffn_kernel(tiles_per_seq, x_ref, g_ref, wg_ref, wv_ref, cwg_ref, cwv_ref, cbg_ref, cbv_ref,
                wd_ref, pg_ref, o_ref, h_ref, ug_ref, uv_ref, carry_g_ref, carry_v_ref):
    tm = FFN_TM
    i = pl.program_id(0)
    j = pl.program_id(1)
    first_in_seq = (i % tiles_per_seq) == 0
    y_ref = o_ref

    @pl.when(j == 0)
    def _():
        x = x_ref[...]
        h_ref[...] = ((x * _rms_scale(x)) * g_ref[...]).astype(BF16)

    @pl.when(jnp.logical_and(i == 0, j == 0))
    def _():
        y_ref[...] = jnp.zeros_like(y_ref)

    sub = tm // FFN_ROW_SPLIT

    def up(r, w_ref, buf_ref):
        rows = slice(r * sub, (r + 1) * sub)
        buf_ref[CONV_HALO + r * sub:CONV_HALO + (r + 1) * sub, :] = jnp.dot(
            h_ref[rows, :], w_ref[...], preferred_element_type=F32)

    def conv(r, cw_ref, cb_ref, buf_ref):
        out = cb_ref[...]
        for tap in range(CONV_WIDTH):
            start = CONV_HALO - (CONV_WIDTH - 1) + tap + r * sub
            out = out + buf_ref[start:start + sub, :] * cw_ref[tap:tap + 1, :]
        return out

    ug_ref[0:CONV_HALO, :] = jnp.where(first_in_seq, 0.0, carry_g_ref[j])
    uv_ref[0:CONV_HALO, :] = jnp.where(first_in_seq, 0.0, carry_v_ref[j])
    for r in range(FFN_ROW_SPLIT):
        up(r, wg_ref, ug_ref)
        up(r, wv_ref, uv_ref)
    carry_g_ref[j] = ug_ref[tm:tm + CONV_HALO, :]
    carry_v_ref[j] = uv_ref[tm:tm + CONV_HALO, :]
    for r in range(FFN_ROW_SPLIT):
        rows = slice(r * sub, (r + 1) * sub)
        gate = conv(r, cwg_ref, cbg_ref, ug_ref)
        val = conv(r, cwv_ref, cbv_ref, uv_ref)
        act = (_gelu_tanh(gate) * val).astype(BF16)
        acc = jnp.where(j == 0, 0.0, y_ref[rows, :])
        y_ref[rows, :] = acc + jnp.dot(act, wd_ref[...], preferred_element_type=F32)

    @pl.when(j == pl.num_programs(1) - 1)
    def _():
        for c in range(tm // NORM_ROWS):
            rows = slice(c * NORM_ROWS, (c + 1) * NORM_ROWS)
            yy = y_ref[rows, :]
            o_ref[rows, :] = x_ref[rows, :] + (yy * _rms_scale(yy)) * pg_ref[...]


def _ffn(layer, x, g, w_up, conv_w, conv_b, w_down, post_g, seq_len):
    t = x.shape[0]
    tm, tf = FFN_TM, FFN_TF
    n_f = D_FF // tf
    tiles_per_seq = seq_len // tm
    return pl.pallas_call(
        functools.partial(_ffn_kernel, tiles_per_seq),
        grid=(t // tm, n_f),
        in_specs=[
            pl.BlockSpec((tm, D_MODEL), lambda i, j: (i, 0)),
            pl.BlockSpec((None, 1, D_MODEL), lambda i, j: (layer, 0, 0)),
            pl.BlockSpec((None, D_MODEL, tf), lambda i, j: (layer, 0, j)),
            pl.BlockSpec((None, D_MODEL, tf), lambda i, j: (layer, 0, j + n_f)),
            pl.BlockSpec((None, CONV_WIDTH, tf), lambda i, j: (layer, 0, j)),
            pl.BlockSpec((None, CONV_WIDTH, tf), lambda i, j: (layer, 0, j + n_f)),
            pl.BlockSpec((None, 1, tf), lambda i, j: (layer, 0, j)),
            pl.BlockSpec((None, 1, tf), lambda i, j: (layer, 0, j + n_f)),
            pl.BlockSpec((None, tf, D_MODEL), lambda i, j: (layer, j, 0)),
            pl.BlockSpec((None, 1, D_MODEL), lambda i, j: (layer, 0, 0)),
        ],
        out_specs=pl.BlockSpec((tm, D_MODEL), lambda i, j: (i, 0)),
        out_shape=jax.ShapeDtypeStruct((t, D_MODEL), F32),
        scratch_shapes=[
            pltpu.VMEM((tm, D_MODEL), BF16),
            pltpu.VMEM((tm + CONV_HALO, tf), F32),
            pltpu.VMEM((tm + CONV_HALO, tf), F32),
            pltpu.VMEM((n_f, CONV_HALO, tf), F32),
            pltpu.VMEM((n_f, CONV_HALO, tf), F32),
        ],
        compiler_params=pltpu.CompilerParams(
            dimension_semantics=("arbitrary", "arbitrary"),
            vmem_limit_bytes=VMEM_LIMIT_BYTES),
        name="ffn",
    )(x, g, w_up, w_up, conv_w, conv_w, conv_b, conv_b, w_down, post_g)


def kernel(x, pre_mix_g, w_in, rel_bias, pool_w, pool_scale, w_o, post_mix_g,
           pre_ffn_g, w_up, conv_w, conv_b, w_down, post_ffn_g):
    b, s, d = x.shape
    depth = w_in.shape[0]
    w_in, pool_w, w_o, w_up, w_down = (w.astype(BF16) for w in (w_in, pool_w, w_o, w_up, w_down))
    pre_mix_g, pool_scale, post_mix_g, pre_ffn_g, conv_b, post_ffn_g = (
        _row(v) for v in (pre_mix_g, pool_scale, post_mix_g, pre_ffn_g, conv_b, post_ffn_g))
    xt = x.reshape(b * s, d)
    for l in range(depth):
        z, u = _inproj(l, xt, pre_mix_g, w_in)
        a = _attention(z, _band_bias(rel_bias[l]), s)
        xt = _mixout(l, xt, a, u, pool_w, pool_scale, w_o, post_mix_g, s)
        xt = _ffn(l, xt, pre_ffn_g, w_up, conv_w, conv_b, w_down, post_ffn_g, s)
    return xt.reshape(b, s, d)
```

```python
import functools
import math

import jax
import jax.numpy as jnp
from jax import lax
from jax.experimental import pallas as pl
from jax.experimental.pallas import tpu as pltpu

F32 = jnp.float32
BF16 = jnp.bfloat16

D_MODEL = 2048
CHUNK = 64
LEFT_CHUNKS = 8
ATTN_WIDTH = 1024
POOL_WIDTH = 1024
HEAD_DIM = 128
N_HEADS = ATTN_WIDTH // HEAD_DIM
REL_CLIP = 128
POOL_WINDOWS = (2, 4, 8, 16)
POOL_GROUP = POOL_WIDTH // len(POOL_WINDOWS)
IN_WIDTH = 3 * ATTN_WIDTH + POOL_WIDTH
D_FF = 5632
CONV_WIDTH = 3
NORM_EPS = 1e-6
MASK_VALUE = -1e30
LOG2_E = math.log2(math.e)
QK_SCALE = LOG2_E / math.sqrt(HEAD_DIM)
Q_COL, K_COL, V_COL = 0, ATTN_WIDTH, 2 * ATTN_WIDTH

V7X_VMEM_BYTES = 64 * 1024 * 1024
VMEM_LIMIT_BYTES = V7X_VMEM_BYTES - 1 * 1024 * 1024

IN_TM = 1024
IN_TN = 2048
ATT_BQ = 4 * CHUNK
ATT_NKB = LEFT_CHUNKS * CHUNK // ATT_BQ + 1
ATT_BK = ATT_NKB * ATT_BQ
MIX_TM = 512
MIX_ROW_SPLIT = 2
POOL_HALO = 16
FFN_TM = 1024
FFN_TF = 512
FFN_ROW_SPLIT = 2
CONV_HALO = 8
NORM_ROWS = 16


def _rms_scale(x):
    return lax.rsqrt(jnp.mean(x * x, axis=-1, keepdims=True) + NORM_EPS)


def _row(vec):
    return vec.reshape(vec.shape[0], 1, vec.shape[1])


def _inproj_kernel(x_ref, g_ref, w_ref, z_ref, u_ref, h_ref):
    j = pl.program_id(1)

    @pl.when(j == 0)
    def _():
        x = x_ref[...]
        h_ref[...] = ((x * _rms_scale(x)) * g_ref[...]).astype(BF16)

    z = jnp.dot(h_ref[...], w_ref[...], preferred_element_type=F32)
    q_scale = jnp.where(j == 0, QK_SCALE, 1.0).astype(F32)
    z_ref[:, :ATTN_WIDTH] = (z[:, :ATTN_WIDTH] * q_scale).astype(BF16)
    z_ref[:, ATTN_WIDTH:] = z[:, ATTN_WIDTH:].astype(BF16)
    u_ref[...] = z[:, IN_TN - POOL_WIDTH:]


def _inproj(layer, x, g, w):
    t = x.shape[0]
    grid = (t // IN_TM, IN_WIDTH // IN_TN)
    return pl.pallas_call(
        _inproj_kernel,
        grid=grid,
        in_specs=[
            pl.BlockSpec((IN_TM, D_MODEL), lambda i, j: (i, 0)),
            pl.BlockSpec((None, 1, D_MODEL), lambda i, j: (layer, 0, 0)),
            pl.BlockSpec((None, D_MODEL, IN_TN), lambda i, j: (layer, 0, j)),
        ],
        out_specs=[
            pl.BlockSpec((IN_TM, IN_TN), lambda i, j: (i, j)),
            pl.BlockSpec((IN_TM, POOL_WIDTH), lambda i, j: (i, 0)),
        ],
        out_shape=[
            jax.ShapeDtypeStruct((t, IN_WIDTH), BF16),
            jax.ShapeDtypeStruct((t, POOL_WIDTH), F32),
        ],
        scratch_shapes=[pltpu.VMEM((IN_TM, D_MODEL), BF16)],
        compiler_params=pltpu.CompilerParams(
            dimension_semantics=("arbitrary", "arbitrary"),
            vmem_limit_bytes=VMEM_LIMIT_BYTES),
        name="inproj",
    )(x, g, w)


def _band_bias(table):
    n_heads = table.shape[0]
    band_keys = (LEFT_CHUNKS + 1) * CHUNK
    far = band_keys - REL_CLIP
    near = REL_CLIP + CHUNK - 1
    table = table.astype(F32)
    ext = jnp.concatenate(
        [jnp.broadcast_to(table[:, 2 * REL_CLIP:], (n_heads, far)),
         table[:, 2 * REL_CLIP - near:2 * REL_CLIP][:, ::-1]], axis=1)
    rows = jnp.stack([ext[:, CHUNK - 1 - qi:CHUNK - 1 - qi + band_keys] for qi in range(CHUNK)], axis=1)
    rows = rows * LOG2_E
    pad = ATT_BK - band_keys
    masked = jnp.full((n_heads, CHUNK, pad), MASK_VALUE, F32)
    blocks = [jnp.concatenate([masked[:, :, :c * CHUNK], rows, masked[:, :, :pad - c * CHUNK]], axis=2)
              for c in range(ATT_BQ // CHUNK)]
    bias = jnp.concatenate(blocks, axis=1)
    key_block = lax.broadcasted_iota(jnp.int32, (ATT_NKB, 1, 1, ATT_BK), 3) // ATT_BQ
    n_masked = lax.broadcasted_iota(jnp.int32, (ATT_NKB, 1, 1, ATT_BK), 0)
    return jnp.where(key_block < n_masked, F32(MASK_VALUE), bias[None])


def _attn_kernel(kvq_ref, bias_ref, o_ref, kt_ref, v_ref, p_ref, linv_ref):
    @pl.when(pl.program_id(0) == 0)
    def _():
        kt_ref[...] = jnp.zeros_like(kt_ref)
        v_ref[...] = jnp.zeros_like(v_ref)
        p_ref[...] = jnp.zeros_like(p_ref)
        linv_ref[...] = jnp.zeros_like(linv_ref)

    v_ref[0:ATT_BK, :] = v_ref[ATT_BQ:ATT_BK + ATT_BQ, :]
    v_ref[ATT_BK:ATT_BK + ATT_BQ, :] = kvq_ref[:, V_COL:V_COL + ATTN_WIDTH]
    kt_ref[:, 0:ATT_BK] = kt_ref[:, ATT_BQ:ATT_BK + ATT_BQ]
    kt_ref[:, ATT_BK:ATT_BK + ATT_BQ] = kvq_ref[:, K_COL:K_COL + ATTN_WIDTH].T

    for h in range(N_HEADS):
        hs = slice(h * HEAD_DIM, (h + 1) * HEAD_DIM)
        o = jnp.dot(p_ref[h], v_ref[0:ATT_BK, hs], preferred_element_type=F32)
        o_ref[:, hs] = (o * linv_ref[h]).astype(BF16)

    for h in range(N_HEADS):
        hs = slice(h * HEAD_DIM, (h + 1) * HEAD_DIM)
        qs = slice(Q_COL + h * HEAD_DIM, Q_COL + (h + 1) * HEAD_DIM)
        s = jnp.dot(kvq_ref[:, qs], kt_ref[hs, ATT_BQ:ATT_BK + ATT_BQ],
                    preferred_element_type=F32)
        s = s + bias_ref[h]
        m = jnp.max(s, axis=-1, keepdims=True)
        p = jnp.exp2(s - m)
        l = jnp.sum(p, axis=-1, keepdims=True)
        p_ref[h] = p.astype(BF16)
        linv_ref[h] = jnp.broadcast_to(1.0 / l, (ATT_BQ, HEAD_DIM))


def _attention(z, bias, seq_len):
    t = z.shape[0]
    n_blocks = t // ATT_BQ
    blocks_per_seq = seq_len // ATT_BQ

    def cur_block(i):
        return jnp.minimum(i, n_blocks - 1)

    def bias_map(i):
        return (jnp.maximum(ATT_NKB - 1 - cur_block(i) % blocks_per_seq, 0), 0, 0, 0)

    return pl.pallas_call(
        _attn_kernel,
        grid=(n_blocks + 1,),
        in_specs=[pl.BlockSpec((ATT_BQ, 3 * ATTN_WIDTH), lambda i: (cur_block(i), 0)),
                  pl.BlockSpec((None, N_HEADS, ATT_BQ, ATT_BK), bias_map)],
        out_specs=pl.BlockSpec((ATT_BQ, ATTN_WIDTH), lambda i: (jnp.maximum(i - 1, 0), 0)),
        out_shape=jax.ShapeDtypeStruct((t, ATTN_WIDTH), BF16),
        scratch_shapes=[pltpu.VMEM((ATTN_WIDTH, ATT_BK + ATT_BQ), BF16),
                        pltpu.VMEM((ATT_BK + ATT_BQ, ATTN_WIDTH), BF16),
                        pltpu.VMEM((N_HEADS, ATT_BQ, ATT_BK), BF16),
                        pltpu.VMEM((N_HEADS, ATT_BQ, HEAD_DIM), F32)],
        compiler_params=pltpu.CompilerParams(
            dimension_semantics=("arbitrary",),
            vmem_limit_bytes=VMEM_LIMIT_BYTES),
        name="attention",
    )(z, bias)


def _mixout_kernel(tiles_per_seq, x_ref, a_ref, u_ref, uh_ref, pw_ref, ps_ref, woa_ref, wop_ref,
                   g_ref, o_ref, ebuf_ref, p_ref, m_ref):
    tm = MIX_TM
    sub = tm // MIX_ROW_SPLIT
    i_in_seq = pl.program_id(0) % tiles_per_seq

    ebuf_ref[0:POOL_HALO, :] = jnp.where(i_in_seq == 0, 0.0, uh_ref[...])
    ebuf_ref[POOL_HALO:, :] = u_ref[...]

    n_groups = len(POOL_WINDOWS)
    a_cols = D_MODEL // n_groups
    head_pos = i_in_seq * tm + lax.broadcasted_iota(jnp.int32, (POOL_HALO, POOL_GROUP), 0)

    def attn_cols(c):
        cols = slice(c * a_cols, (c + 1) * a_cols)
        m_ref[:, cols] = jnp.dot(a_ref[...], woa_ref[:, cols], preferred_element_type=F32)

    for g, w in enumerate(POOL_WINDOWS):
        if g > 0:
            attn_cols(g - 1)
        if g == n_groups - 1:
            attn_cols(g)

        cs = slice(g * POOL_GROUP, (g + 1) * POOL_GROUP)
        tok = ebuf_ref[POOL_HALO:, cs]
        wsum = tok
        for k in range(1, w):
            wsum = wsum + ebuf_ref[POOL_HALO - k:POOL_HALO - k + tm, cs]
        body = wsum * (1.0 / w) - tok
        head_count = jnp.minimum(head_pos + 1, w).astype(F32)
        head = wsum[:POOL_HALO] / head_count - tok[:POOL_HALO]
        pooled = jnp.concatenate([head, body[POOL_HALO:]], axis=0).astype(BF16)
        mixed = jnp.dot(pooled, pw_ref[g], preferred_element_type=F32)
        p_ref[:, cs] = (mixed * ps_ref[:, cs]).astype(BF16)

    for r in range(MIX_ROW_SPLIT):
        rows = slice(r * sub, (r + 1) * sub)
        m = m_ref[rows, :] + jnp.dot(p_ref[rows, :], wop_ref[...], preferred_element_type=F32)
        o_ref[rows, :] = x_ref[rows, :] + (m * _rms_scale(m)) * g_ref[...]


def _mixout(layer, x, a, u, pool_w, pool_scale, w_o, g, seq_len):
    t = x.shape[0]
    tm = MIX_TM
    tiles_per_seq = seq_len // tm
    halo_blocks = tm // POOL_HALO
    n_groups = len(POOL_WINDOWS)
    return pl.pallas_call(
        functools.partial(_mixout_kernel, tiles_per_seq),
        grid=(t // tm,),
        in_specs=[
            pl.BlockSpec((tm, D_MODEL), lambda i: (i, 0)),
            pl.BlockSpec((tm, ATTN_WIDTH), lambda i: (i, 0)),
            pl.BlockSpec((tm, POOL_WIDTH), lambda i: (i, 0)),
            pl.BlockSpec((POOL_HALO, POOL_WIDTH), lambda i: (jnp.maximum(i * halo_blocks - 1, 0), 0)),
            pl.BlockSpec((None, n_groups, POOL_GROUP, POOL_GROUP), lambda i: (layer, 0, 0, 0)),
            pl.BlockSpec((None, 1, POOL_WIDTH), lambda i: (layer, 0, 0)),
            pl.BlockSpec((None, ATTN_WIDTH, D_MODEL), lambda i: (layer, 0, 0)),
            pl.BlockSpec((None, POOL_WIDTH, D_MODEL), lambda i: (layer, 1, 0)),
            pl.BlockSpec((None, 1, D_MODEL), lambda i: (layer, 0, 0)),
        ],
        out_specs=pl.BlockSpec((tm, D_MODEL), lambda i: (i, 0)),
        out_shape=jax.ShapeDtypeStruct((t, D_MODEL), F32),
        scratch_shapes=[pltpu.VMEM((tm + POOL_HALO, POOL_WIDTH), F32),
                        pltpu.VMEM((tm, POOL_WIDTH), BF16),
                        pltpu.VMEM((tm, D_MODEL), F32)],
        compiler_params=pltpu.CompilerParams(
            dimension_semantics=("arbitrary",),
            vmem_limit_bytes=VMEM_LIMIT_BYTES),
        name="mixout",
    )(x, a, u, u, pool_w, pool_scale, w_o, w_o, g)


def _gelu_tanh(x):
    return 0.5 * x * (1.0 + jnp.tanh(math.sqrt(2.0 / math.pi) * (x + 0.044715 * (x * x * x))))


def _ffn_kernel(tiles_per_seq, x_ref, g_ref, wg_ref, wv_ref, cwg_ref, cwv_ref, cbg_ref, cbv_ref,
                wd_ref, pg_ref, o_ref, h_ref, ug_ref, uv_ref, carry_g_ref, carry_v_ref):
    tm = FFN_TM
    i = pl.program_id(0)
    j = pl.program_id(1)
    first_in_seq = (i % tiles_per_seq) == 0
    y_ref = o_ref

    @pl.when(j == 0)
    def _():
        x = x_ref[...]
        h_ref[...] = ((x * _rms_scale(x)) * g_ref[...]).astype(BF16)

    @pl.when(jnp.logical_and(i == 0, j == 0))
    def _():
        y_ref[...] = jnp.zeros_like(y_ref)

    sub = tm // FFN_ROW_SPLIT

    def up(r, w_ref, buf_ref):
        rows = slice(r * sub, (r + 1) * sub)
        buf_ref[CONV_HALO + r * sub:CONV_HALO + (r + 1) * sub, :] = jnp.dot(
            h_ref[rows, :], w_ref[...], preferred_element_type=F32)

    def conv(r, cw_ref, cb_ref, buf_ref):
        out = cb_ref[...]
        for tap in range(CONV_WIDTH):
            start = CONV_HALO - (CONV_WIDTH - 1) + tap + r * sub
            out = out + buf_ref[start:start + sub, :] * cw_ref[tap:tap + 1, :]
        return out

    ug_ref[0:CONV_HALO, :] = jnp.where(first_in_seq, 0.0, carry_g_ref[j])
    uv_ref[0:CONV_HALO, :] = jnp.where(first_in_seq, 0.0, carry_v_ref[j])
    for r in range(FFN_ROW_SPLIT):
        up(r, wg_ref, ug_ref)
        up(r, wv_ref, uv_ref)
    carry_g_ref[j] = ug_ref[tm:tm + CONV_HALO, :]
    carry_v_ref[j] = uv_ref[tm:tm + CONV_HALO, :]
    for r in range(FFN_ROW_SPLIT):
        rows = slice(r * sub, (r + 1) * sub)
        gate = conv(r, cwg_ref, cbg_ref, ug_ref)
        val = conv(r, cwv_ref, cbv_ref, uv_ref)
        act = (_gelu_tanh(gate) * val).astype(BF16)
        acc = jnp.where(j == 0, 0.0, y_ref[rows, :])
        y_ref[rows, :] = acc + jnp.dot(act, wd_ref[...], preferred_element_type=F32)

    @pl.when(j == pl.num_programs(1) - 1)
    def _():
        for c in range(tm // NORM_ROWS):
            rows = slice(c * NORM_ROWS, (c + 1) * NORM_ROWS)
            yy = y_ref[rows, :]
            o_ref[rows, :] = x_ref[rows, :] + (yy * _rms_scale(yy)) * pg_ref[...]


def _ffn(layer, x, g, w_up, conv_w, conv_b, w_down, post_g, seq_len):
    t = x.shape[0]
    tm, tf = FFN_TM, FFN_TF
    n_f = D_FF // tf
    tiles_per_seq = seq_len // tm
    return pl.pallas_call(
        functools.partial(_ffn_kernel, tiles_per_seq),
        grid=(t // tm, n_f),
        in_specs=[
            pl.BlockSpec((tm, D_MODEL), lambda i, j: (i, 0)),
            pl.BlockSpec((None, 1, D_MODEL), lambda i, j: (layer, 0, 0)),
            pl.BlockSpec((None, D_MODEL, tf), lambda i, j: (layer, 0, j)),
            pl.BlockSpec((None, D_MODEL, tf), lambda i, j: (layer, 0, j + n_f)),
            pl.BlockSpec((None, CONV_WIDTH, tf), lambda i, j: (layer, 0, j)),
            pl.BlockSpec((None, CONV_WIDTH, tf), lambda i, j: (layer, 0, j + n_f)),
            pl.BlockSpec((None, 1, tf), lambda i, j: (layer, 0, j)),
            pl.BlockSpec((None, 1, tf), lambda i, j: (layer, 0, j + n_f)),
            pl.BlockSpec((None, tf, D_MODEL), lambda i, j: (layer, j, 0)),
            pl.BlockSpec((None, 1, D_MODEL), lambda i, j: (layer, 0, 0)),
        ],
        out_specs=pl.BlockSpec((tm, D_MODEL), lambda i, j: (i, 0)),
        out_shape=jax.ShapeDtypeStruct((t, D_MODEL), F32),
        scratch_shapes=[
            pltpu.VMEM((tm, D_MODEL), BF16),
            pltpu.VMEM((tm + CONV_HALO, tf), F32),
            pltpu.VMEM((tm + CONV_HALO, tf), F32),
            pltpu.VMEM((n_f, CONV_HALO, tf), F32),
            pltpu.VMEM((n_f, CONV_HALO, tf), F32),
        ],
        compiler_params=pltpu.CompilerParams(
            dimension_semantics=("arbitrary", "arbitrary"),
            vmem_limit_bytes=VMEM_LIMIT_BYTES),
        name="ffn",
    )(x, g, w_up, w_up, conv_w, conv_w, conv_b, conv_b, w_down, post_g)


def kernel(x, pre_mix_g, w_in, rel_bias, pool_w, pool_scale, w_o, post_mix_g,
           pre_ffn_g, w_up, conv_w, conv_b, w_down, post_ffn_g):
    b, s, d = x.shape
    depth = w_in.shape[0]
    w_in, pool_w, w_o, w_up, w_down = (w.astype(BF16) for w in (w_in, pool_w, w_o, w_up, w_down))
    pre_mix_g, pool_scale, post_mix_g, pre_ffn_g, conv_b, post_ffn_g = (
        _row(v) for v in (pre_mix_g, pool_scale, post_mix_g, pre_ffn_g, conv_b, post_ffn_g))
    xt = x.reshape(b * s, d)
    for l in range(depth):
        z, u = _inproj(l, xt, pre_mix_g, w_in)
        a = _attention(z, _band_bias(rel_bias[l]), s)
        xt = _mixout(l, xt, a, u, pool_w, pool_scale, w_o, post_mix_g, s)
        xt = _ffn(l, xt, pre_ffn_g, w_up, conv_w, conv_b, w_down, post_ffn_g, s)
    return xt.reshape(b, s, d)
```

```python
import functools
import math

import jax
import jax.numpy as jnp
from jax import lax
from jax.experimental import pallas as pl
from jax.experimental.pallas import tpu as pltpu

F32 = jnp.float32
BF16 = jnp.bfloat16

D_MODEL = 2048
CHUNK = 64
LEFT_CHUNKS = 8
ATTN_WIDTH = 1024
POOL_WIDTH = 1024
HEAD_DIM = 128
N_HEADS = ATTN_WIDTH // HEAD_DIM
REL_CLIP = 128
POOL_WINDOWS = (2, 4, 8, 16)
POOL_GROUP = POOL_WIDTH // len(POOL_WINDOWS)
IN_WIDTH = 3 * ATTN_WIDTH + POOL_WIDTH
D_FF = 5632
CONV_WIDTH = 3
NORM_EPS = 1e-6
MASK_VALUE = -1e30
LOG2_E = math.log2(math.e)
QK_SCALE = LOG2_E / math.sqrt(HEAD_DIM)
Q_COL, K_COL, V_COL = 0, ATTN_WIDTH, 2 * ATTN_WIDTH

V7X_VMEM_BYTES = 64 * 1024 * 1024
VMEM_LIMIT_BYTES = V7X_VMEM_BYTES - 1 * 1024 * 1024

IN_TM = 1024
IN_TN = 2048
ATT_BQ = 4 * CHUNK
ATT_NKB = LEFT_CHUNKS * CHUNK // ATT_BQ + 1
ATT_BK = ATT_NKB * ATT_BQ
ATT_HALF = 2 * CHUNK
ATT_BAND = (LEFT_CHUNKS + 1) * CHUNK + ATT_HALF - CHUNK
MIX_TM = 512
MIX_ROW_SPLIT = 2
POOL_HALO = 16
FFN_TM = 1024
FFN_TF = 512
FFN_ROW_SPLIT = 2
CONV_HALO = 8
NORM_ROWS = 16


def _rms_scale(x):
    return lax.rsqrt(jnp.mean(x * x, axis=-1, keepdims=True) + NORM_EPS)


def _row(vec):
    return vec.reshape(vec.shape[0], 1, vec.shape[1])


def _inproj_kernel(x_ref, g_ref, w_ref, z_ref, u_ref, h_ref):
    j = pl.program_id(1)

    @pl.when(j == 0)
    def _():
        x = x_ref[...]
        h_ref[...] = ((x * _rms_scale(x)) * g_ref[...]).astype(BF16)

    z = jnp.dot(h_ref[...], w_ref[...], preferred_element_type=F32)
    q_scale = jnp.where(j == 0, QK_SCALE, 1.0).astype(F32)
    z_ref[:, :ATTN_WIDTH] = (z[:, :ATTN_WIDTH] * q_scale).astype(BF16)
    z_ref[:, ATTN_WIDTH:] = z[:, ATTN_WIDTH:].astype(BF16)
    u_ref[...] = z[:, IN_TN - POOL_WIDTH:]


def _inproj(layer, x, g, w):
    t = x.shape[0]
    grid = (t // IN_TM, IN_WIDTH // IN_TN)
    return pl.pallas_call(
        _inproj_kernel,
        grid=grid,
        in_specs=[
            pl.BlockSpec((IN_TM, D_MODEL), lambda i, j: (i, 0)),
            pl.BlockSpec((None, 1, D_MODEL), lambda i, j: (layer, 0, 0)),
            pl.BlockSpec((None, D_MODEL, IN_TN), lambda i, j: (layer, 0, j)),
        ],
        out_specs=[
            pl.BlockSpec((IN_TM, IN_TN), lambda i, j: (i, j)),
            pl.BlockSpec((IN_TM, POOL_WIDTH), lambda i, j: (i, 0)),
        ],
        out_shape=[
            jax.ShapeDtypeStruct((t, IN_WIDTH), BF16),
            jax.ShapeDtypeStruct((t, POOL_WIDTH), F32),
        ],
        scratch_shapes=[pltpu.VMEM((IN_TM, D_MODEL), BF16)],
        compiler_params=pltpu.CompilerParams(
            dimension_semantics=("arbitrary", "arbitrary"),
            vmem_limit_bytes=VMEM_LIMIT_BYTES),
        name="inproj",
    )(x, g, w)


def _band_bias(table):
    n_heads = table.shape[0]
    band_keys = (LEFT_CHUNKS + 1) * CHUNK
    far = band_keys - REL_CLIP
    near = REL_CLIP + CHUNK - 1
    table = table.astype(F32)
    ext = jnp.concatenate(
        [jnp.broadcast_to(table[:, 2 * REL_CLIP:], (n_heads, far)),
         table[:, 2 * REL_CLIP - near:2 * REL_CLIP][:, ::-1]], axis=1)
    rows = jnp.stack([ext[:, CHUNK - 1 - qi:CHUNK - 1 - qi + band_keys] for qi in range(CHUNK)], axis=1)
    rows = rows * LOG2_E
    pad = ATT_BK - band_keys
    masked = jnp.full((n_heads, CHUNK, pad), MASK_VALUE, F32)
    blocks = [jnp.concatenate([masked[:, :, :c * CHUNK], rows, masked[:, :, :pad - c * CHUNK]], axis=2)
              for c in range(ATT_BQ // CHUNK)]
    bias = jnp.concatenate(blocks, axis=1)
    key_block = lax.broadcasted_iota(jnp.int32, (ATT_NKB, 1, 1, ATT_BK), 3) // ATT_BQ
    n_masked = lax.broadcasted_iota(jnp.int32, (ATT_NKB, 1, 1, ATT_BK), 0)
    return jnp.where(key_block < n_masked, F32(MASK_VALUE), bias[None])


def _attn_kernel(kvq_ref, bias_ref, o_ref, kt_ref, v_ref, p_ref, linv_ref):
    @pl.when(pl.program_id(0) == 0)
    def _():
        kt_ref[...] = jnp.zeros_like(kt_ref)
        v_ref[...] = jnp.zeros_like(v_ref)
        p_ref[...] = jnp.zeros_like(p_ref)
        linv_ref[...] = jnp.zeros_like(linv_ref)

    v_ref[0:ATT_BK, :] = v_ref[ATT_BQ:ATT_BK + ATT_BQ, :]
    v_ref[ATT_BK:ATT_BK + ATT_BQ, :] = kvq_ref[:, V_COL:V_COL + ATTN_WIDTH]
    kt_ref[:, 0:ATT_BK] = kt_ref[:, ATT_BQ:ATT_BK + ATT_BQ]
    kt_ref[:, ATT_BK:ATT_BK + ATT_BQ] = kvq_ref[:, K_COL:K_COL + ATTN_WIDTH].T

    for h in range(N_HEADS):
        hs = slice(h * HEAD_DIM, (h + 1) * HEAD_DIM)
        o = jnp.dot(p_ref[h], v_ref[0:ATT_BK, hs], preferred_element_type=F32)
        o_ref[:, hs] = (o * linv_ref[h]).astype(BF16)

    for h in range(N_HEADS):
        hs = slice(h * HEAD_DIM, (h + 1) * HEAD_DIM)
        qs = slice(Q_COL + h * HEAD_DIM, Q_COL + (h + 1) * HEAD_DIM)
        s = jnp.dot(kvq_ref[:, qs], kt_ref[hs, ATT_BQ:ATT_BK + ATT_BQ],
                    preferred_element_type=F32)
        for half in range(ATT_BQ // ATT_HALF):
            rows = slice(half * ATT_HALF, (half + 1) * ATT_HALF)
            cols = slice(half * ATT_HALF, half * ATT_HALF + ATT_BAND)
            sh = s[rows, cols] + bias_ref[h, rows, cols]
            m = jnp.max(sh, axis=-1, keepdims=True)
            p = jnp.exp2(sh - m)
            l = jnp.sum(p, axis=-1, keepdims=True)
            p_ref[h, rows, cols] = p.astype(BF16)
            linv_ref[h, rows, :] = jnp.broadcast_to(1.0 / l, (ATT_HALF, HEAD_DIM))


def _attention(z, bias, seq_len):
    t = z.shape[0]
    n_blocks = t // ATT_BQ
    blocks_per_seq = seq_len // ATT_BQ

    def cur_block(i):
        return jnp.minimum(i, n_blocks - 1)

    def bias_map(i):
        return (jnp.maximum(ATT_NKB - 1 - cur_block(i) % blocks_per_seq, 0), 0, 0, 0)

    return pl.pallas_call(
        _attn_kernel,
        grid=(n_blocks + 1,),
        in_specs=[pl.BlockSpec((ATT_BQ, 3 * ATTN_WIDTH), lambda i: (cur_block(i), 0)),
                  pl.BlockSpec((None, N_HEADS, ATT_BQ, ATT_BK), bias_map)],
        out_specs=pl.BlockSpec((ATT_BQ, ATTN_WIDTH), lambda i: (jnp.maximum(i - 1, 0), 0)),
        out_shape=jax.ShapeDtypeStruct((t, ATTN_WIDTH), BF16),
        scratch_shapes=[pltpu.VMEM((ATTN_WIDTH, ATT_BK + ATT_BQ), BF16),
                        pltpu.VMEM((ATT_BK + ATT_BQ, ATTN_WIDTH), BF16),
                        pltpu.VMEM((N_HEADS, ATT_BQ, ATT_BK), BF16),
                        pltpu.VMEM((N_HEADS, ATT_BQ, HEAD_DIM), F32)],
        compiler_params=pltpu.CompilerParams(
            dimension_semantics=("arbitrary",),
            vmem_limit_bytes=VMEM_LIMIT_BYTES),
        name="attention",
    )(z, bias)


def _mixout_kernel(tiles_per_seq, x_ref, a_ref, u_ref, uh_ref, pw_ref, ps_ref, woa_ref, wop_ref,
                   g_ref, o_ref, ebuf_ref, p_ref, m_ref):
    tm = MIX_TM
    sub = tm // MIX_ROW_SPLIT
    i_in_seq = pl.program_id(0) % tiles_per_seq

    ebuf_ref[0:POOL_HALO, :] = jnp.where(i_in_seq == 0, 0.0, uh_ref[...])
    ebuf_ref[POOL_HALO:, :] = u_ref[...]

    n_groups = len(POOL_WINDOWS)
    a_cols = D_MODEL // n_groups
    head_pos = i_in_seq * tm + lax.broadcasted_iota(jnp.int32, (POOL_HALO, POOL_GROUP), 0)

    def attn_cols(c):
        cols = slice(c * a_cols, (c + 1) * a_cols)
        m_ref[:, cols] = jnp.dot(a_ref[...], woa_ref[:, cols], preferred_element_type=F32)

    for g, w in enumerate(POOL_WINDOWS):
        if g > 0:
            attn_cols(g - 1)
        if g == n_groups - 1:
            attn_cols(g)

        cs = slice(g * POOL_GROUP, (g + 1) * POOL_GROUP)
        tok = ebuf_ref[POOL_HALO:, cs]
        wsum = tok
        for k in range(1, w):
            wsum = wsum + ebuf_ref[POOL_HALO - k:POOL_HALO - k + tm, cs]
        body = wsum * (1.0 / w) - tok
        head_count = jnp.minimum(head_pos + 1, w).astype(F32)
        head = wsum[:POOL_HALO] / head_count - tok[:POOL_HALO]
        pooled = jnp.concatenate([head, body[POOL_HALO:]], axis=0).astype(BF16)
        mixed = jnp.dot(pooled, pw_ref[g], preferred_element_type=F32)
        p_ref[:, cs] = (mixed * ps_ref[:, cs]).astype(BF16)

    for r in range(MIX_ROW_SPLIT):
        rows = slice(r * sub, (r + 1) * sub)
        m = m_ref[rows, :] + jnp.dot(p_ref[rows, :], wop_ref[...], preferred_element_type=F32)
        o_ref[rows, :] = x_ref[rows, :] + (m * _rms_scale(m)) * g_ref[...]


def _mixout(layer, x, a, u, pool_w, pool_scale, w_o, g, seq_len):
    t = x.shape[0]
    tm = MIX_TM
    tiles_per_seq = seq_len // tm
    halo_blocks = tm // POOL_HALO
    n_groups = len(POOL_WINDOWS)
    return pl.pallas_call(
        functools.partial(_mixout_kernel, tiles_per_seq),
        grid=(t // tm,),
        in_specs=[
            pl.BlockSpec((tm, D_MODEL), lambda i: (i, 0)),
            pl.BlockSpec((tm, ATTN_WIDTH), lambda i: (i, 0)),
            pl.BlockSpec((tm, POOL_WIDTH), lambda i: (i, 0)),
            pl.BlockSpec((POOL_HALO, POOL_WIDTH), lambda i: (jnp.maximum(i * halo_blocks - 1, 0), 0)),
            pl.BlockSpec((None, n_groups, POOL_GROUP, POOL_GROUP), lambda i: (layer, 0, 0, 0)),
            pl.BlockSpec((None, 1, POOL_WIDTH), lambda i: (layer, 0, 0)),
            pl.BlockSpec((None, ATTN_WIDTH, D_MODEL), lambda i: (layer, 0, 0)),
            pl.BlockSpec((None, POOL_WIDTH, D_MODEL), lambda i: (layer, 1, 0)),
            pl.BlockSpec((None, 1, D_MODEL), lambda i: (layer, 0, 0)),
        ],
        out_specs=pl.BlockSpec((tm, D_MODEL), lambda i: (i, 0)),
        out_shape=jax.ShapeDtypeStruct((t, D_MODEL), F32),
        scratch_shapes=[pltpu.VMEM((tm + POOL_HALO, POOL_WIDTH), F32),
                        pltpu.VMEM((tm, POOL_WIDTH), BF16),
                        pltpu.VMEM((tm, D_MODEL), F32)],
        compiler_params=pltpu.CompilerParams(
            dimension_semantics=("arbitrary",),
            vmem_limit_bytes=VMEM_LIMIT_BYTES),
        name="mixout",
    )(x, a, u, u, pool_w, pool_scale, w_o, w_o, g)


def _gelu_tanh(x):
    return 0.5 * x * (1.0 + jnp.tanh(math.sqrt(2.0 / math.pi) * (x + 0.044715 * (x * x * x))))


def _ffn_kernel(tiles_per_seq, x_ref, g_ref, wg_ref, wv_ref, cw_ref, cb_ref,
                wd_ref, pg_ref, o_ref, h_ref, ug_ref, uv_ref, carry_g_ref, carry_v_ref):
    tm = FFN_TM
    tf = FFN_TF
    i = pl.program_id(0)
    j = pl.program_id(1)
    first_in_seq = (i % tiles_per_seq) == 0
    y_ref = o_ref

    @pl.when(j == 0)
    def _():
        x = x_ref[...]
        h_ref[...] = ((x * _rms_scale(x)) * g_ref[...]).astype(BF16)

    @pl.when(jnp.logical_and(i == 0, j == 0))
    def _():
        y_ref[...] = jnp.zeros_like(y_ref)

    sub = tm // FFN_ROW_SPLIT

    def up(r, w_ref, buf_ref):
        rows = slice(r * sub, (r + 1) * sub)
        buf_ref[CONV_HALO + r * sub:CONV_HALO + (r + 1) * sub, :] = jnp.dot(
            h_ref[rows, :], w_ref[...], preferred_element_type=F32)

    gate_cols = pl.ds(pl.multiple_of(j * tf, tf), tf)
    val_cols = pl.ds(pl.multiple_of(D_FF + j * tf, tf), tf)

    def conv(r, cols, buf_ref):
        out = cb_ref[:, cols]
        for tap in range(CONV_WIDTH):
            start = CONV_HALO - (CONV_WIDTH - 1) + tap + r * sub
            out = out + buf_ref[start:start + sub, :] * cw_ref[tap:tap + 1, cols]
        return out

    ug_ref[0:CONV_HALO, :] = jnp.where(first_in_seq, 0.0, carry_g_ref[j])
    uv_ref[0:CONV_HALO, :] = jnp.where(first_in_seq, 0.0, carry_v_ref[j])
    for r in range(FFN_ROW_SPLIT):
        up(r, wg_ref, ug_ref)
        up(r, wv_ref, uv_ref)
    carry_g_ref[j] = ug_ref[tm:tm + CONV_HALO, :]
    carry_v_ref[j] = uv_ref[tm:tm + CONV_HALO, :]
    for r in range(FFN_ROW_SPLIT):
        rows = slice(r * sub, (r + 1) * sub)
        gate = conv(r, gate_cols, ug_ref)
        val = conv(r, val_cols, uv_ref)
        act = (_gelu_tanh(gate) * val).astype(BF16)
        acc = jnp.where(j == 0, 0.0, y_ref[rows, :])
        y_ref[rows, :] = acc + jnp.dot(act, wd_ref[...], preferred_element_type=F32)

    @pl.when(j == pl.num_programs(1) - 1)
    def _():
        for c in range(tm // NORM_ROWS):
            rows = slice(c * NORM_ROWS, (c + 1) * NORM_ROWS)
            yy = y_ref[rows, :]
            o_ref[rows, :] = x_ref[rows, :] + (yy * _rms_scale(yy)) * pg_ref[...]


def _ffn(layer, x, g, w_up, conv_w, conv_b, w_down, post_g, seq_len):
    t = x.shape[0]
    tm, tf = FFN_TM, FFN_TF
    n_f = D_FF // tf
    tiles_per_seq = seq_len // tm
    return pl.pallas_call(
        functools.partial(_ffn_kernel, tiles_per_seq),
        grid=(t // tm, n_f),
        in_specs=[
            pl.BlockSpec((tm, D_MODEL), lambda i, j: (i, 0)),
            pl.BlockSpec((None, 1, D_MODEL), lambda i, j: (layer, 0, 0)),
            pl.BlockSpec((None, D_MODEL, tf), lambda i, j: (layer, 0, j)),
            pl.BlockSpec((None, D_MODEL, tf), lambda i, j: (layer, 0, j + n_f)),
            pl.BlockSpec((None, CONV_WIDTH, 2 * D_FF), lambda i, j: (layer, 0, 0)),
            pl.BlockSpec((None, 1, 2 * D_FF), lambda i, j: (layer, 0, 0)),
            pl.BlockSpec((None, tf, D_MODEL), lambda i, j: (layer, j, 0)),
            pl.BlockSpec((None, 1, D_MODEL), lambda i, j: (layer, 0, 0)),
        ],
        out_specs=pl.BlockSpec((tm, D_MODEL), lambda i, j: (i, 0)),
        out_shape=jax.ShapeDtypeStruct((t, D_MODEL), F32),
        scratch_shapes=[
            pltpu.VMEM((tm, D_MODEL), BF16),
            pltpu.VMEM((tm + CONV_HALO, tf), F32),
            pltpu.VMEM((tm + CONV_HALO, tf), F32),
            pltpu.VMEM((n_f, CONV_HALO, tf), F32),
            pltpu.VMEM((n_f, CONV_HALO, tf), F32),
        ],
        compiler_params=pltpu.CompilerParams(
            dimension_semantics=("arbitrary", "arbitrary"),
            vmem_limit_bytes=VMEM_LIMIT_BYTES),
        name="ffn",
    )(x, g, w_up, w_up, conv_w, conv_b, w_down, post_g)


def kernel(x, pre_mix_g, w_in, rel_bias, pool_w, pool_scale, w_o, post_mix_g,
           pre_ffn_g, w_up, conv_w, conv_b, w_down, post_ffn_g):
    b, s, d = x.shape
    depth = w_in.shape[0]
    w_in, pool_w, w_o, w_up, w_down = (w.astype(BF16) for w in (w_in, pool_w, w_o, w_up, w_down))
    pre_mix_g, pool_scale, post_mix_g, pre_ffn_g, conv_b, post_ffn_g = (
        _row(v) for v in (pre_mix_g, pool_scale, post_mix_g, pre_ffn_g, conv_b, post_ffn_g))
    xt = x.reshape(b * s, d)
    for l in range(depth):
        z, u = _inproj(l, xt, pre_mix_g, w_in)
        a = _attention(z, _band_bias(rel_bias[l]), s)
        xt = _mixout(l, xt, a, u, pool_w, pool_scale, w_o, post_mix_g, s)
        xt = _ffn(l, xt, pre_ffn_g, w_up, conv_w, conv_b, w_down, post_ffn_g, s)
    return xt.reshape(b, s, d)
```

```python
import functools
import math

import jax
import jax.numpy as jnp
from jax import lax
from jax.experimental import pallas as pl
from jax.experimental.pallas import tpu as pltpu

F32 = jnp.float32
BF16 = jnp.bfloat16

D_MODEL = 2048
CHUNK = 64
LEFT_CHUNKS = 8
ATTN_WIDTH = 1024
POOL_WIDTH = 1024
HEAD_DIM = 128
N_HEADS = ATTN_WIDTH // HEAD_DIM
REL_CLIP = 128
POOL_WINDOWS = (2, 4, 8, 16)
POOL_GROUP = POOL_WIDTH // len(POOL_WINDOWS)
IN_WIDTH = 3 * ATTN_WIDTH + POOL_WIDTH
D_FF = 5632
CONV_WIDTH = 3
NORM_EPS = 1e-6
MASK_VALUE = -1e30
LOG2_E = math.log2(math.e)
QK_SCALE = LOG2_E / math.sqrt(HEAD_DIM)
Q_COL, K_COL, V_COL = 0, ATTN_WIDTH, 2 * ATTN_WIDTH

V7X_VMEM_BYTES = 64 * 1024 * 1024
VMEM_RESERVE_BYTES = 1024 * 1024
VMEM_LIMIT_BYTES = V7X_VMEM_BYTES - VMEM_RESERVE_BYTES

IN_TM = 1024
IN_TN = 2048
ATT_BQ = 4 * CHUNK
ATT_NKB = LEFT_CHUNKS * CHUNK // ATT_BQ + 1
ATT_BK = ATT_NKB * ATT_BQ
ATT_HALF = 2 * CHUNK
ATT_BAND = (LEFT_CHUNKS + 1) * CHUNK + ATT_HALF - CHUNK
MIX_TM = 512
MIX_ROW_SPLIT = 2
POOL_HALO = 16
FFN_TM = 1024
FFN_TF = 512
FFN_ROW_SPLIT = 2
CONV_HALO = 8
NORM_ROWS = 16


def _rms_scale(x):
    return lax.rsqrt(jnp.mean(x * x, axis=-1, keepdims=True) + NORM_EPS)


def _row(vec):
    return vec.reshape(vec.shape[0], 1, vec.shape[1])


def _inproj_kernel(x_ref, g_ref, w_ref, z_ref, u_ref, h_ref):
    j = pl.program_id(1)

    @pl.when(j == 0)
    def _():
        x = x_ref[...]
        h_ref[...] = ((x * _rms_scale(x)) * g_ref[...]).astype(BF16)

    z = jnp.dot(h_ref[...], w_ref[...], preferred_element_type=F32)
    q_scale = jnp.where(j == 0, QK_SCALE, 1.0).astype(F32)
    z_ref[:, :ATTN_WIDTH] = (z[:, :ATTN_WIDTH] * q_scale).astype(BF16)
    z_ref[:, ATTN_WIDTH:] = z[:, ATTN_WIDTH:].astype(BF16)
    u_ref[...] = z[:, IN_TN - POOL_WIDTH:]


def _inproj(layer, x, g, w):
    t = x.shape[0]
    grid = (t // IN_TM, IN_WIDTH // IN_TN)
    return pl.pallas_call(
        _inproj_kernel,
        grid=grid,
        in_specs=[
            pl.BlockSpec((IN_TM, D_MODEL), lambda i, j: (i, 0)),
            pl.BlockSpec((None, 1, D_MODEL), lambda i, j: (layer, 0, 0)),
            pl.BlockSpec((None, D_MODEL, IN_TN), lambda i, j: (layer, 0, j)),
        ],
        out_specs=[
            pl.BlockSpec((IN_TM, IN_TN), lambda i, j: (i, j)),
            pl.BlockSpec((IN_TM, POOL_WIDTH), lambda i, j: (i, 0)),
        ],
        out_shape=[
            jax.ShapeDtypeStruct((t, IN_WIDTH), BF16),
            jax.ShapeDtypeStruct((t, POOL_WIDTH), F32),
        ],
        scratch_shapes=[pltpu.VMEM((IN_TM, D_MODEL), BF16)],
        compiler_params=pltpu.CompilerParams(
            dimension_semantics=("arbitrary", "arbitrary"),
            vmem_limit_bytes=VMEM_LIMIT_BYTES),
        name="inproj",
    )(x, g, w)


def _band_bias(table):
    n_heads = table.shape[0]
    band_keys = (LEFT_CHUNKS + 1) * CHUNK
    far = band_keys - REL_CLIP
    near = REL_CLIP + CHUNK - 1
    table = table.astype(F32)
    ext = jnp.concatenate(
        [jnp.broadcast_to(table[:, 2 * REL_CLIP:], (n_heads, far)),
         table[:, 2 * REL_CLIP - near:2 * REL_CLIP][:, ::-1]], axis=1)
    rows = jnp.stack([ext[:, CHUNK - 1 - qi:CHUNK - 1 - qi + band_keys] for qi in range(CHUNK)], axis=1)
    rows = rows * LOG2_E
    pad = ATT_BK - band_keys
    masked = jnp.full((n_heads, CHUNK, pad), MASK_VALUE, F32)
    blocks = [jnp.concatenate([masked[:, :, :c * CHUNK], rows, masked[:, :, :pad - c * CHUNK]], axis=2)
              for c in range(ATT_BQ // CHUNK)]
    bias = jnp.concatenate(blocks, axis=1)
    key_block = lax.broadcasted_iota(jnp.int32, (ATT_NKB, 1, 1, ATT_BK), 3) // ATT_BQ
    n_masked = lax.broadcasted_iota(jnp.int32, (ATT_NKB, 1, 1, ATT_BK), 0)
    return jnp.where(key_block < n_masked, F32(MASK_VALUE), bias[None])


def _attn_kernel(kvq_ref, bias_ref, o_ref, kt_ref, v_ref, p_ref, linv_ref):
    @pl.when(pl.program_id(0) == 0)
    def _():
        kt_ref[...] = jnp.zeros_like(kt_ref)
        v_ref[...] = jnp.zeros_like(v_ref)
        p_ref[...] = jnp.zeros_like(p_ref)
        linv_ref[...] = jnp.zeros_like(linv_ref)

    v_ref[0:ATT_BK, :] = v_ref[ATT_BQ:ATT_BK + ATT_BQ, :]
    v_ref[ATT_BK:ATT_BK + ATT_BQ, :] = kvq_ref[:, V_COL:V_COL + ATTN_WIDTH]
    kt_ref[:, 0:ATT_BK] = kt_ref[:, ATT_BQ:ATT_BK + ATT_BQ]
    kt_ref[:, ATT_BK:ATT_BK + ATT_BQ] = kvq_ref[:, K_COL:K_COL + ATTN_WIDTH].T

    for h in range(N_HEADS):
        hs = slice(h * HEAD_DIM, (h + 1) * HEAD_DIM)
        o = jnp.dot(p_ref[h], v_ref[0:ATT_BK, hs], preferred_element_type=F32)
        o_ref[:, hs] = (o * linv_ref[h]).astype(BF16)

    for h in range(N_HEADS):
        hs = slice(h * HEAD_DIM, (h + 1) * HEAD_DIM)
        qs = slice(Q_COL + h * HEAD_DIM, Q_COL + (h + 1) * HEAD_DIM)
        s = jnp.dot(kvq_ref[:, qs], kt_ref[hs, ATT_BQ:ATT_BK + ATT_BQ],
                    preferred_element_type=F32)
        for half in range(ATT_BQ // ATT_HALF):
            rows = slice(half * ATT_HALF, (half + 1) * ATT_HALF)
            cols = slice(half * ATT_HALF, half * ATT_HALF + ATT_BAND)
            sh = s[rows, cols] + bias_ref[h, rows, cols]
            m = jnp.max(sh, axis=-1, keepdims=True)
            p = jnp.exp2(sh - m)
            l = jnp.sum(p, axis=-1, keepdims=True)
            p_ref[h, rows, cols] = p.astype(BF16)
            linv_ref[h, rows, :] = jnp.broadcast_to(1.0 / l, (ATT_HALF, HEAD_DIM))


def _attention(z, bias, seq_len):
    t = z.shape[0]
    n_blocks = t // ATT_BQ
    blocks_per_seq = seq_len // ATT_BQ

    def cur_block(i):
        return jnp.minimum(i, n_blocks - 1)

    def bias_map(i):
        return (jnp.maximum(ATT_NKB - 1 - cur_block(i) % blocks_per_seq, 0), 0, 0, 0)

    return pl.pallas_call(
        _attn_kernel,
        grid=(n_blocks + 1,),
        in_specs=[pl.BlockSpec((ATT_BQ, 3 * ATTN_WIDTH), lambda i: (cur_block(i), 0)),
                  pl.BlockSpec((None, N_HEADS, ATT_BQ, ATT_BK), bias_map)],
        out_specs=pl.BlockSpec((ATT_BQ, ATTN_WIDTH), lambda i: (jnp.maximum(i - 1, 0), 0)),
        out_shape=jax.ShapeDtypeStruct((t, ATTN_WIDTH), BF16),
        scratch_shapes=[pltpu.VMEM((ATTN_WIDTH, ATT_BK + ATT_BQ), BF16),
                        pltpu.VMEM((ATT_BK + ATT_BQ, ATTN_WIDTH), BF16),
                        pltpu.VMEM((N_HEADS, ATT_BQ, ATT_BK), BF16),
                        pltpu.VMEM((N_HEADS, ATT_BQ, HEAD_DIM), F32)],
        compiler_params=pltpu.CompilerParams(
            dimension_semantics=("arbitrary",),
            vmem_limit_bytes=VMEM_LIMIT_BYTES),
        name="attention",
    )(z, bias)


def _mixout_kernel(tiles_per_seq, x_ref, a_ref, u_ref, uh_ref, pw_ref, ps_ref, woa_ref, wop_ref,
                   g_ref, o_ref, ebuf_ref, p_ref, m_ref):
    tm = MIX_TM
    sub = tm // MIX_ROW_SPLIT
    i_in_seq = pl.program_id(0) % tiles_per_seq

    ebuf_ref[0:POOL_HALO, :] = jnp.where(i_in_seq == 0, 0.0, uh_ref[...])
    ebuf_ref[POOL_HALO:, :] = u_ref[...]

    n_groups = len(POOL_WINDOWS)
    a_cols = D_MODEL // n_groups
    head_pos = i_in_seq * tm + lax.broadcasted_iota(jnp.int32, (POOL_HALO, POOL_GROUP), 0)

    def attn_cols(c):
        cols = slice(c * a_cols, (c + 1) * a_cols)
        m_ref[:, cols] = jnp.dot(a_ref[...], woa_ref[:, cols], preferred_element_type=F32)

    for g, w in enumerate(POOL_WINDOWS):
        if g > 0:
            attn_cols(g - 1)
        if g == n_groups - 1:
            attn_cols(g)

        cs = slice(g * POOL_GROUP, (g + 1) * POOL_GROUP)
        tok = ebuf_ref[POOL_HALO:, cs]
        wsum = tok
        for k in range(1, w):
            wsum = wsum + ebuf_ref[POOL_HALO - k:POOL_HALO - k + tm, cs]
        body = wsum * (1.0 / w) - tok
        head_count = jnp.minimum(head_pos + 1, w).astype(F32)
        head = wsum[:POOL_HALO] / head_count - tok[:POOL_HALO]
        pooled = jnp.concatenate([head, body[POOL_HALO:]], axis=0).astype(BF16)
        mixed = jnp.dot(pooled, pw_ref[g], preferred_element_type=F32)
        p_ref[:, cs] = (mixed * ps_ref[:, cs]).astype(BF16)

    for r in range(MIX_ROW_SPLIT):
        rows = slice(r * sub, (r + 1) * sub)
        m = m_ref[rows, :] + jnp.dot(p_ref[rows, :], wop_ref[...], preferred_element_type=F32)
        o_ref[rows, :] = x_ref[rows, :] + (m * _rms_scale(m)) * g_ref[...]


def _mixout(layer, x, a, u, pool_w, pool_scale, w_o, g, seq_len):
    t = x.shape[0]
    tm = MIX_TM
    tiles_per_seq = seq_len // tm
    halo_blocks = tm // POOL_HALO
    n_groups = len(POOL_WINDOWS)
    return pl.pallas_call(
        functools.partial(_mixout_kernel, tiles_per_seq),
        grid=(t // tm,),
        in_specs=[
            pl.BlockSpec((tm, D_MODEL), lambda i: (i, 0)),
            pl.BlockSpec((tm, ATTN_WIDTH), lambda i: (i, 0)),
            pl.BlockSpec((tm, POOL_WIDTH), lambda i: (i, 0)),
            pl.BlockSpec((POOL_HALO, POOL_WIDTH), lambda i: (jnp.maximum(i * halo_blocks - 1, 0), 0)),
            pl.BlockSpec((None, n_groups, POOL_GROUP, POOL_GROUP), lambda i: (layer, 0, 0, 0)),
            pl.BlockSpec((None, 1, POOL_WIDTH), lambda i: (layer, 0, 0)),
            pl.BlockSpec((None, ATTN_WIDTH, D_MODEL), lambda i: (layer, 0, 0)),
            pl.BlockSpec((None, POOL_WIDTH, D_MODEL), lambda i: (layer, 1, 0)),
            pl.BlockSpec((None, 1, D_MODEL), lambda i: (layer, 0, 0)),
        ],
        out_specs=pl.BlockSpec((tm, D_MODEL), lambda i: (i, 0)),
        out_shape=jax.ShapeDtypeStruct((t, D_MODEL), F32),
        scratch_shapes=[pltpu.VMEM((tm + POOL_HALO, POOL_WIDTH), F32),
                        pltpu.VMEM((tm, POOL_WIDTH), BF16),
                        pltpu.VMEM((tm, D_MODEL), F32)],
        compiler_params=pltpu.CompilerParams(
            dimension_semantics=("arbitrary",),
            vmem_limit_bytes=VMEM_LIMIT_BYTES),
        name="mixout",
    )(x, a, u, u, pool_w, pool_scale, w_o, w_o, g)


def _gelu_tanh(x):
    return 0.5 * x * (1.0 + jnp.tanh(math.sqrt(2.0 / math.pi) * (x + 0.044715 * (x * x * x))))


def _ffn_kernel(tiles_per_seq, x_ref, g_ref, wg_ref, wv_ref, cw_ref, cb_ref,
                wd_ref, pg_ref, o_ref, h_ref, ug_ref, uv_ref, carry_g_ref, carry_v_ref):
    tm = FFN_TM
    tf = FFN_TF
    i = pl.program_id(0)
    j = pl.program_id(1)
    first_in_seq = (i % tiles_per_seq) == 0
    y_ref = o_ref

    @pl.when(j == 0)
    def _():
        x = x_ref[...]
        h_ref[...] = ((x * _rms_scale(x)) * g_ref[...]).astype(BF16)

    @pl.when(jnp.logical_and(i == 0, j == 0))
    def _():
        y_ref[...] = jnp.zeros_like(y_ref)

    sub = tm // FFN_ROW_SPLIT

    def up(r, w_ref, buf_ref):
        rows = slice(r * sub, (r + 1) * sub)
        buf_ref[CONV_HALO + r * sub:CONV_HALO + (r + 1) * sub, :] = jnp.dot(
            h_ref[rows, :], w_ref[...], preferred_element_type=F32)

    gate_cols = pl.ds(pl.multiple_of(j * tf, tf), tf)
    val_cols = pl.ds(pl.multiple_of(D_FF + j * tf, tf), tf)

    def conv(r, cols, buf_ref):
        out = cb_ref[:, cols]
        for tap in range(CONV_WIDTH):
            start = CONV_HALO - (CONV_WIDTH - 1) + tap + r * sub
            out = out + buf_ref[start:start + sub, :] * cw_ref[tap:tap + 1, cols]
        return out

    ug_ref[0:CONV_HALO, :] = jnp.where(first_in_seq, 0.0, carry_g_ref[j])
    uv_ref[0:CONV_HALO, :] = jnp.where(first_in_seq, 0.0, carry_v_ref[j])
    for r in range(FFN_ROW_SPLIT):
        up(r, wg_ref, ug_ref)
        up(r, wv_ref, uv_ref)
    carry_g_ref[j] = ug_ref[tm:tm + CONV_HALO, :]
    carry_v_ref[j] = uv_ref[tm:tm + CONV_HALO, :]
    for r in range(FFN_ROW_SPLIT):
        rows = slice(r * sub, (r + 1) * sub)
        gate = conv(r, gate_cols, ug_ref)
        val = conv(r, val_cols, uv_ref)
        act = (_gelu_tanh(gate) * val).astype(BF16)
        acc = jnp.where(j == 0, 0.0, y_ref[rows, :])
        y_ref[rows, :] = acc + jnp.dot(act, wd_ref[...], preferred_element_type=F32)

    @pl.when(j == pl.num_programs(1) - 1)
    def _():
        for c in range(tm // NORM_ROWS):
            rows = slice(c * NORM_ROWS, (c + 1) * NORM_ROWS)
            yy = y_ref[rows, :]
            o_ref[rows, :] = x_ref[rows, :] + (yy * _rms_scale(yy)) * pg_ref[...]


def _ffn(layer, x, g, w_up, conv_w, conv_b, w_down, post_g, seq_len):
    t = x.shape[0]
    tm, tf = FFN_TM, FFN_TF
    n_f = D_FF // tf
    tiles_per_seq = seq_len // tm
    return pl.pallas_call(
        functools.partial(_ffn_kernel, tiles_per_seq),
        grid=(t // tm, n_f),
        in_specs=[
            pl.BlockSpec((tm, D_MODEL), lambda i, j: (i, 0)),
            pl.BlockSpec((None, 1, D_MODEL), lambda i, j: (layer, 0, 0)),
            pl.BlockSpec((None, D_MODEL, tf), lambda i, j: (layer, 0, j)),
            pl.BlockSpec((None, D_MODEL, tf), lambda i, j: (layer, 0, j + n_f)),
            pl.BlockSpec((None, CONV_WIDTH, 2 * D_FF), lambda i, j: (layer, 0, 0)),
            pl.BlockSpec((None, 1, 2 * D_FF), lambda i, j: (layer, 0, 0)),
            pl.BlockSpec((None, tf, D_MODEL), lambda i, j: (layer, j, 0)),
            pl.BlockSpec((None, 1, D_MODEL), lambda i, j: (layer, 0, 0)),
        ],
        out_specs=pl.BlockSpec((tm, D_MODEL), lambda i, j: (i, 0)),
        out_shape=jax.ShapeDtypeStruct((t, D_MODEL), F32),
        scratch_shapes=[
            pltpu.VMEM((tm, D_MODEL), BF16),
            pltpu.VMEM((tm + CONV_HALO, tf), F32),
            pltpu.VMEM((tm + CONV_HALO, tf), F32),
            pltpu.VMEM((n_f, CONV_HALO, tf), F32),
            pltpu.VMEM((n_f, CONV_HALO, tf), F32),
        ],
        compiler_params=pltpu.CompilerParams(
            dimension_semantics=("arbitrary", "arbitrary"),
            vmem_limit_bytes=VMEM_LIMIT_BYTES),
        name="ffn",
    )(x, g, w_up, w_up, conv_w, conv_b, w_down, post_g)


def kernel(x, pre_mix_g, w_in, rel_bias, pool_w, pool_scale, w_o, post_mix_g,
           pre_ffn_g, w_up, conv_w, conv_b, w_down, post_ffn_g):
    b, s, d = x.shape
    depth = w_in.shape[0]
    assert d == D_MODEL and w_in.shape[1:] == (D_MODEL, IN_WIDTH) and w_up.shape[1:] == (D_MODEL, 2 * D_FF)
    assert all(s % tile == 0 for tile in (IN_TM, ATT_BQ, MIX_TM, FFN_TM)) and s // ATT_BQ >= ATT_NKB
    w_in, pool_w, w_o, w_up, w_down = (w.astype(BF16) for w in (w_in, pool_w, w_o, w_up, w_down))
    pre_mix_g, pool_scale, post_mix_g, pre_ffn_g, conv_b, post_ffn_g = (
        _row(v) for v in (pre_mix_g, pool_scale, post_mix_g, pre_ffn_g, conv_b, post_ffn_g))
    xt = x.reshape(b * s, d)
    for l in range(depth):
        z, u = _inproj(l, xt, pre_mix_g, w_in)
        a = _attention(z, _band_bias(rel_bias[l]), s)
        xt = _mixout(l, xt, a, u, pool_w, pool_scale, w_o, post_mix_g, s)
        xt = _ffn(l, xt, pre_ffn_g, w_up, conv_w, conv_b, w_down, post_ffn_g, s)
    return xt.reshape(b, s, d)
```

```python
import functools
import math

import jax
import jax.numpy as jnp
from jax import lax
from jax.experimental import pallas as pl
from jax.experimental.pallas import tpu as pltpu

F32 = jnp.float32
BF16 = jnp.bfloat16

D_MODEL = 2048
CHUNK = 64
LEFT_CHUNKS = 8
ATTN_WIDTH = 1024
POOL_WIDTH = 1024
HEAD_DIM = 128
N_HEADS = ATTN_WIDTH // HEAD_DIM
REL_CLIP = 128
POOL_WINDOWS = (2, 4, 8, 16)
POOL_GROUP = POOL_WIDTH // len(POOL_WINDOWS)
IN_WIDTH = 3 * ATTN_WIDTH + POOL_WIDTH
D_FF = 5632
CONV_WIDTH = 3
NORM_EPS = 1e-6
MASK_VALUE = -1e30
LOG2_E = math.log2(math.e)
QK_SCALE = LOG2_E / math.sqrt(HEAD_DIM)
Q_COL, K_COL, V_COL = 0, ATTN_WIDTH, 2 * ATTN_WIDTH

V7X_VMEM_BYTES = 64 * 1024 * 1024
VMEM_RESERVE_BYTES = 1024 * 1024
VMEM_LIMIT_BYTES = V7X_VMEM_BYTES - VMEM_RESERVE_BYTES

IN_TM = 1024
IN_TN = 2048
ATT_BQ = 4 * CHUNK
ATT_NKB = LEFT_CHUNKS * CHUNK // ATT_BQ + 1
ATT_BK = ATT_NKB * ATT_BQ
ATT_HALF = 2 * CHUNK
ATT_BAND = (LEFT_CHUNKS + 1) * CHUNK + ATT_HALF - CHUNK
MIX_TM = 512
MIX_ROW_SPLIT = 2
POOL_HALO = 16
FFN_TM = 1024
FFN_TF = 512
FFN_ROW_SPLIT = 2
CONV_HALO = 8
NORM_ROWS = 16


def _rms_scale(x):
    return lax.rsqrt(jnp.mean(x * x, axis=-1, keepdims=True) + NORM_EPS)


def _row(vec):
    return vec.reshape(vec.shape[0], 1, vec.shape[1])


def _inproj_kernel(x_ref, g_ref, w_ref, z_ref, u_ref, h_ref):
    j = pl.program_id(1)

    @pl.when(j == 0)
    def _():
        x = x_ref[...]
        h_ref[...] = ((x * _rms_scale(x)) * g_ref[...]).astype(BF16)

    z = jnp.dot(h_ref[...], w_ref[...], preferred_element_type=F32)
    q_scale = jnp.where(j == 0, QK_SCALE, 1.0).astype(F32)
    z_ref[:, :ATTN_WIDTH] = (z[:, :ATTN_WIDTH] * q_scale).astype(BF16)
    z_ref[:, ATTN_WIDTH:] = z[:, ATTN_WIDTH:].astype(BF16)
    u_ref[...] = z[:, IN_TN - POOL_WIDTH:]


def _inproj(layer, x, g, w):
    t = x.shape[0]
    grid = (t // IN_TM, IN_WIDTH // IN_TN)
    return pl.pallas_call(
        _inproj_kernel,
        grid=grid,
        in_specs=[
            pl.BlockSpec((IN_TM, D_MODEL), lambda i, j: (i, 0)),
            pl.BlockSpec((None, 1, D_MODEL), lambda i, j: (layer, 0, 0)),
            pl.BlockSpec((None, D_MODEL, IN_TN), lambda i, j: (layer, 0, j)),
        ],
        out_specs=[
            pl.BlockSpec((IN_TM, IN_TN), lambda i, j: (i, j)),
            pl.BlockSpec((IN_TM, POOL_WIDTH), lambda i, j: (i, 0)),
        ],
        out_shape=[
            jax.ShapeDtypeStruct((t, IN_WIDTH), BF16),
            jax.ShapeDtypeStruct((t, POOL_WIDTH), F32),
        ],
        scratch_shapes=[pltpu.VMEM((IN_TM, D_MODEL), BF16)],
        compiler_params=pltpu.CompilerParams(
            dimension_semantics=("arbitrary", "arbitrary"),
            vmem_limit_bytes=VMEM_LIMIT_BYTES),
        name="inproj",
    )(x, g, w)


def _band_bias(table):
    n_heads = table.shape[0]
    band_keys = (LEFT_CHUNKS + 1) * CHUNK
    far = band_keys - REL_CLIP
    near = REL_CLIP + CHUNK - 1
    table = table.astype(F32)
    ext = jnp.concatenate(
        [jnp.broadcast_to(table[:, 2 * REL_CLIP:], (n_heads, far)),
         table[:, 2 * REL_CLIP - near:2 * REL_CLIP][:, ::-1]], axis=1)
    rows = jnp.stack([ext[:, CHUNK - 1 - qi:CHUNK - 1 - qi + band_keys] for qi in range(CHUNK)], axis=1)
    rows = rows * LOG2_E
    pad = ATT_BK - band_keys
    masked = jnp.full((n_heads, CHUNK, pad), MASK_VALUE, F32)
    blocks = [jnp.concatenate([masked[:, :, :c * CHUNK], rows, masked[:, :, :pad - c * CHUNK]], axis=2)
              for c in range(ATT_BQ // CHUNK)]
    bias = jnp.concatenate(blocks, axis=1)
    key_block = lax.broadcasted_iota(jnp.int32, (ATT_NKB, 1, 1, ATT_BK), 3) // ATT_BQ
    n_masked = lax.broadcasted_iota(jnp.int32, (ATT_NKB, 1, 1, ATT_BK), 0)
    return jnp.where(key_block < n_masked, F32(MASK_VALUE), bias[None])


def _attn_kernel(kvq_ref, bias_ref, o_ref, kt_ref, v_ref, p_ref, linv_ref):
    @pl.when(pl.program_id(0) == 0)
    def _():
        kt_ref[...] = jnp.zeros_like(kt_ref)
        v_ref[...] = jnp.zeros_like(v_ref)
        p_ref[...] = jnp.zeros_like(p_ref)
        linv_ref[...] = jnp.zeros_like(linv_ref)

    v_ref[0:ATT_BK, :] = v_ref[ATT_BQ:ATT_BK + ATT_BQ, :]
    v_ref[ATT_BK:ATT_BK + ATT_BQ, :] = kvq_ref[:, V_COL:V_COL + ATTN_WIDTH]
    kt_ref[:, 0:ATT_BK] = kt_ref[:, ATT_BQ:ATT_BK + ATT_BQ]
    kt_ref[:, ATT_BK:ATT_BK + ATT_BQ] = kvq_ref[:, K_COL:K_COL + ATTN_WIDTH].T

    for h in range(N_HEADS):
        hs = slice(h * HEAD_DIM, (h + 1) * HEAD_DIM)
        o = jnp.dot(p_ref[h], v_ref[0:ATT_BK, hs], preferred_element_type=F32)
        o_ref[:, hs] = (o * linv_ref[h]).astype(BF16)

    for h in range(N_HEADS):
        hs = slice(h * HEAD_DIM, (h + 1) * HEAD_DIM)
        qs = slice(Q_COL + h * HEAD_DIM, Q_COL + (h + 1) * HEAD_DIM)
        s = jnp.dot(kvq_ref[:, qs], kt_ref[hs, ATT_BQ:ATT_BK + ATT_BQ],
                    preferred_element_type=F32)
        for half in range(ATT_BQ // ATT_HALF):
            rows = slice(half * ATT_HALF, (half + 1) * ATT_HALF)
            cols = slice(half * ATT_HALF, half * ATT_HALF + ATT_BAND)
            sh = s[rows, cols] + bias_ref[h, rows, cols]
            m = jnp.max(sh, axis=-1, keepdims=True)
            p = jnp.exp2(sh - m)
            l = jnp.sum(p, axis=-1, keepdims=True)
            p_ref[h, rows, cols] = p.astype(BF16)
            linv_ref[h, rows, :] = jnp.broadcast_to(1.0 / l, (ATT_HALF, HEAD_DIM))


def _attention(z, bias, seq_len):
    t = z.shape[0]
    n_blocks = t // ATT_BQ
    blocks_per_seq = seq_len // ATT_BQ

    def cur_block(i):
        return jnp.minimum(i, n_blocks - 1)

    def bias_map(i):
        return (jnp.maximum(ATT_NKB - 1 - cur_block(i) % blocks_per_seq, 0), 0, 0, 0)

    return pl.pallas_call(
        _attn_kernel,
        grid=(n_blocks + 1,),
        in_specs=[pl.BlockSpec((ATT_BQ, 3 * ATTN_WIDTH), lambda i: (cur_block(i), 0)),
                  pl.BlockSpec((None, N_HEADS, ATT_BQ, ATT_BK), bias_map)],
        out_specs=pl.BlockSpec((ATT_BQ, ATTN_WIDTH), lambda i: (jnp.maximum(i - 1, 0), 0)),
        out_shape=jax.ShapeDtypeStruct((t, ATTN_WIDTH), BF16),
        scratch_shapes=[pltpu.VMEM((ATTN_WIDTH, ATT_BK + ATT_BQ), BF16),
                        pltpu.VMEM((ATT_BK + ATT_BQ, ATTN_WIDTH), BF16),
                        pltpu.VMEM((N_HEADS, ATT_BQ, ATT_BK), BF16),
                        pltpu.VMEM((N_HEADS, ATT_BQ, HEAD_DIM), F32)],
        compiler_params=pltpu.CompilerParams(
            dimension_semantics=("arbitrary",),
            vmem_limit_bytes=VMEM_LIMIT_BYTES),
        name="attention",
    )(z, bias)


def _mixout_kernel(tiles_per_seq, x_ref, a_ref, u_ref, uh_ref, pw_ref, ps_ref, woa_ref, wop_ref,
                   g_ref, o_ref, ebuf_ref, p_ref, m_ref):
    tm = MIX_TM
    sub = tm // MIX_ROW_SPLIT
    i_in_seq = pl.program_id(0) % tiles_per_seq

    ebuf_ref[0:POOL_HALO, :] = jnp.where(i_in_seq == 0, 0.0, uh_ref[...])
    ebuf_ref[POOL_HALO:, :] = u_ref[...]

    n_groups = len(POOL_WINDOWS)
    a_cols = D_MODEL // n_groups
    head_pos = i_in_seq * tm + lax.broadcasted_iota(jnp.int32, (POOL_HALO, POOL_GROUP), 0)

    def attn_cols(c):
        cols = slice(c * a_cols, (c + 1) * a_cols)
        m_ref[:, cols] = jnp.dot(a_ref[...], woa_ref[:, cols], preferred_element_type=F32)

    for g, w in enumerate(POOL_WINDOWS):
        if g > 0:
            attn_cols(g - 1)
        if g == n_groups - 1:
            attn_cols(g)

        cs = slice(g * POOL_GROUP, (g + 1) * POOL_GROUP)
        tok = ebuf_ref[POOL_HALO:, cs]
        wsum = tok
        for k in range(1, w):
            wsum = wsum + ebuf_ref[POOL_HALO - k:POOL_HALO - k + tm, cs]
        body = wsum * (1.0 / w) - tok
        head_count = jnp.minimum(head_pos + 1, w).astype(F32)
        head = wsum[:POOL_HALO] / head_count - tok[:POOL_HALO]
        pooled = jnp.concatenate([head, body[POOL_HALO:]], axis=0).astype(BF16)
        mixed = jnp.dot(pooled, pw_ref[g], preferred_element_type=F32)
        p_ref[:, cs] = (mixed * ps_ref[:, cs]).astype(BF16)

    for r in range(MIX_ROW_SPLIT):
        rows = slice(r * sub, (r + 1) * sub)
        m = m_ref[rows, :] + jnp.dot(p_ref[rows, :], wop_ref[...], preferred_element_type=F32)
        o_ref[rows, :] = x_ref[rows, :] + (m * _rms_scale(m)) * g_ref[...]


def _mixout(layer, x, a, u, pool_w, pool_scale, w_o, g, seq_len):
    t = x.shape[0]
    tm = MIX_TM
    tiles_per_seq = seq_len // tm
    halo_blocks = tm // POOL_HALO
    n_groups = len(POOL_WINDOWS)
    return pl.pallas_call(
        functools.partial(_mixout_kernel, tiles_per_seq),
        grid=(t // tm,),
        in_specs=[
            pl.BlockSpec((tm, D_MODEL), lambda i: (i, 0)),
            pl.BlockSpec((tm, ATTN_WIDTH), lambda i: (i, 0)),
            pl.BlockSpec((tm, POOL_WIDTH), lambda i: (i, 0)),
            pl.BlockSpec((POOL_HALO, POOL_WIDTH), lambda i: (jnp.maximum(i * halo_blocks - 1, 0), 0)),
            pl.BlockSpec((None, n_groups, POOL_GROUP, POOL_GROUP), lambda i: (layer, 0, 0, 0)),
            pl.BlockSpec((None, 1, POOL_WIDTH), lambda i: (layer, 0, 0)),
            pl.BlockSpec((None, ATTN_WIDTH, D_MODEL), lambda i: (layer, 0, 0)),
            pl.BlockSpec((None, POOL_WIDTH, D_MODEL), lambda i: (layer, 1, 0)),
            pl.BlockSpec((None, 1, D_MODEL), lambda i: (layer, 0, 0)),
        ],
        out_specs=pl.BlockSpec((tm, D_MODEL), lambda i: (i, 0)),
        out_shape=jax.ShapeDtypeStruct((t, D_MODEL), F32),
        scratch_shapes=[pltpu.VMEM((tm + POOL_HALO, POOL_WIDTH), F32),
                        pltpu.VMEM((tm, POOL_WIDTH), BF16),
                        pltpu.VMEM((tm, D_MODEL), F32)],
        compiler_params=pltpu.CompilerParams(
            dimension_semantics=("arbitrary",),
            vmem_limit_bytes=VMEM_LIMIT_BYTES),
        name="mixout",
    )(x, a, u, u, pool_w, pool_scale, w_o, w_o, g)


def _gelu_tanh(x):
    return 0.5 * x * (1.0 + jnp.tanh(math.sqrt(2.0 / math.pi) * (x + 0.044715 * (x * x * x))))


def _ffn_kernel(layer, n_f, tiles_per_seq, x_ref, g_ref, wup_hbm_ref, cw_ref, cb_ref, wdn_hbm_ref,
                pg_ref, o_ref, h_ref, ug_ref, uv_ref, carry_g_ref, carry_v_ref,
                wg_ref, wv_ref, wd_ref, sem):
    tm = FFN_TM
    tf = FFN_TF
    i = pl.program_id(0)
    n_tiles = pl.num_programs(0)
    first_in_seq = (i % tiles_per_seq) == 0
    y_ref = o_ref
    sub = tm // FFN_ROW_SPLIT

    def chunk_copies(j, slot):
        first = pl.multiple_of(j * tf, tf)
        return (
            pltpu.make_async_copy(wup_hbm_ref.at[layer, :, pl.ds(first, tf)], wg_ref.at[slot], sem.at[0, slot]),
            pltpu.make_async_copy(wup_hbm_ref.at[layer, :, pl.ds(D_FF + first, tf)], wv_ref.at[slot],
                                  sem.at[1, slot]),
            pltpu.make_async_copy(wdn_hbm_ref.at[layer, pl.ds(first, tf), :], wd_ref.at[slot], sem.at[2, slot]),
        )

    @pl.when(i == 0)
    def _():
        for copy in chunk_copies(0, 0):
            copy.start()
        y_ref[...] = jnp.zeros_like(y_ref)

    x = x_ref[...]
    h_ref[...] = ((x * _rms_scale(x)) * g_ref[...]).astype(BF16)

    def up(r, w, buf_ref):
        rows = slice(r * sub, (r + 1) * sub)
        buf_ref[CONV_HALO + r * sub:CONV_HALO + (r + 1) * sub, :] = jnp.dot(
            h_ref[rows, :], w, preferred_element_type=F32)

    def conv(r, cols, buf_ref):
        out = cb_ref[:, cols]
        for tap in range(CONV_WIDTH):
            start = CONV_HALO - (CONV_WIDTH - 1) + tap + r * sub
            out = out + buf_ref[start:start + sub, :] * cw_ref[tap:tap + 1, cols]
        return out

    def chunk(j, _):
        slot = (i * n_f + j) % 2
        for copy in chunk_copies(j, slot):
            copy.wait()

        @pl.when(jnp.logical_or(j + 1 < n_f, i + 1 < n_tiles))
        def _():
            for copy in chunk_copies(jnp.where(j + 1 < n_f, j + 1, 0), 1 - slot):
                copy.start()

        gate_cols = pl.ds(pl.multiple_of(j * tf, tf), tf)
        val_cols = pl.ds(pl.multiple_of(D_FF + j * tf, tf), tf)
        ug_ref[0:CONV_HALO, :] = jnp.where(first_in_seq, 0.0, carry_g_ref[j])
        uv_ref[0:CONV_HALO, :] = jnp.where(first_in_seq, 0.0, carry_v_ref[j])
        for r in range(FFN_ROW_SPLIT):
            up(r, wg_ref[slot], ug_ref)
            up(r, wv_ref[slot], uv_ref)
        carry_g_ref[j] = ug_ref[tm:tm + CONV_HALO, :]
        carry_v_ref[j] = uv_ref[tm:tm + CONV_HALO, :]
        for r in range(FFN_ROW_SPLIT):
            rows = slice(r * sub, (r + 1) * sub)
            gate = conv(r, gate_cols, ug_ref)
            val = conv(r, val_cols, uv_ref)
            act = (_gelu_tanh(gate) * val).astype(BF16)
            acc = jnp.where(j == 0, 0.0, y_ref[rows, :])
            y_ref[rows, :] = acc + jnp.dot(act, wd_ref[slot], preferred_element_type=F32)
        return 0

    lax.fori_loop(0, n_f, chunk, 0)

    for c in range(tm // NORM_ROWS):
        rows = slice(c * NORM_ROWS, (c + 1) * NORM_ROWS)
        yy = y_ref[rows, :]
        o_ref[rows, :] = x_ref[rows, :] + (yy * _rms_scale(yy)) * pg_ref[...]


def _ffn(layer, x, g, w_up, conv_w, conv_b, w_down, post_g, seq_len):
    t = x.shape[0]
    tm, tf = FFN_TM, FFN_TF
    n_f = D_FF // tf
    tiles_per_seq = seq_len // tm
    return pl.pallas_call(
        functools.partial(_ffn_kernel, layer, n_f, tiles_per_seq),
        grid=(t // tm,),
        in_specs=[
            pl.BlockSpec((tm, D_MODEL), lambda i: (i, 0)),
            pl.BlockSpec((None, 1, D_MODEL), lambda i: (layer, 0, 0)),
            pl.BlockSpec(memory_space=pl.ANY),
            pl.BlockSpec((None, CONV_WIDTH, 2 * D_FF), lambda i: (layer, 0, 0)),
            pl.BlockSpec((None, 1, 2 * D_FF), lambda i: (layer, 0, 0)),
            pl.BlockSpec(memory_space=pl.ANY),
            pl.BlockSpec((None, 1, D_MODEL), lambda i: (layer, 0, 0)),
        ],
        out_specs=pl.BlockSpec((tm, D_MODEL), lambda i: (i, 0)),
        out_shape=jax.ShapeDtypeStruct((t, D_MODEL), F32),
        scratch_shapes=[
            pltpu.VMEM((tm, D_MODEL), BF16),
            pltpu.VMEM((tm + CONV_HALO, tf), F32),
            pltpu.VMEM((tm + CONV_HALO, tf), F32),
            pltpu.VMEM((n_f, CONV_HALO, tf), F32),
            pltpu.VMEM((n_f, CONV_HALO, tf), F32),
            pltpu.VMEM((2, D_MODEL, tf), BF16),
            pltpu.VMEM((2, D_MODEL, tf), BF16),
            pltpu.VMEM((2, tf, D_MODEL), BF16),
            pltpu.SemaphoreType.DMA((3, 2)),
        ],
        compiler_params=pltpu.CompilerParams(
            dimension_semantics=("arbitrary",),
            vmem_limit_bytes=VMEM_LIMIT_BYTES),
        name="ffn",
    )(x, g, w_up, conv_w, conv_b, w_down, post_g)


def kernel(x, pre_mix_g, w_in, rel_bias, pool_w, pool_scale, w_o, post_mix_g,
           pre_ffn_g, w_up, conv_w, conv_b, w_down, post_ffn_g):
    b, s, d = x.shape
    depth = w_in.shape[0]
    assert d == D_MODEL and w_in.shape[1:] == (D_MODEL, IN_WIDTH) and w_up.shape[1:] == (D_MODEL, 2 * D_FF)
    assert all(s % tile == 0 for tile in (IN_TM, ATT_BQ, MIX_TM, FFN_TM)) and s // ATT_BQ >= ATT_NKB
    w_in, pool_w, w_o, w_up, w_down = (w.astype(BF16) for w in (w_in, pool_w, w_o, w_up, w_down))
    pre_mix_g, pool_scale, post_mix_g, pre_ffn_g, conv_b, post_ffn_g = (
        _row(v) for v in (pre_mix_g, pool_scale, post_mix_g, pre_ffn_g, conv_b, post_ffn_g))
    xt = x.reshape(b * s, d)
    for l in range(depth):
        z, u = _inproj(l, xt, pre_mix_g, w_in)
        a = _attention(z, _band_bias(rel_bias[l]), s)
        xt = _mixout(l, xt, a, u, pool_w, pool_scale, w_o, post_mix_g, s)
        xt = _ffn(l, xt, pre_ffn_g, w_up, conv_w, conv_b, w_down, post_ffn_g, s)
    return xt.reshape(b, s, d)
```

```python
import functools
import math

import jax
import jax.numpy as jnp
from jax import lax
from jax.experimental import pallas as pl
from jax.experimental.pallas import tpu as pltpu

F32 = jnp.float32
BF16 = jnp.bfloat16

D_MODEL = 2048
CHUNK = 64
LEFT_CHUNKS = 8
ATTN_WIDTH = 1024
POOL_WIDTH = 1024
HEAD_DIM = 128
N_HEADS = ATTN_WIDTH // HEAD_DIM
REL_CLIP = 128
POOL_WINDOWS = (2, 4, 8, 16)
POOL_GROUP = POOL_WIDTH // len(POOL_WINDOWS)
IN_WIDTH = 3 * ATTN_WIDTH + POOL_WIDTH
D_FF = 5632
CONV_WIDTH = 3
NORM_EPS = 1e-6
MASK_VALUE = -1e30
LOG2_E = math.log2(math.e)
QK_SCALE = LOG2_E / math.sqrt(HEAD_DIM)
Q_COL, K_COL, V_COL = 0, ATTN_WIDTH, 2 * ATTN_WIDTH

V7X_VMEM_BYTES = 64 * 1024 * 1024
VMEM_RESERVE_BYTES = 1024 * 1024
VMEM_LIMIT_BYTES = V7X_VMEM_BYTES - VMEM_RESERVE_BYTES

IN_TM = 1024
IN_TN = 2048
ATT_BQ = 4 * CHUNK
ATT_NKB = LEFT_CHUNKS * CHUNK // ATT_BQ + 1
ATT_BK = ATT_NKB * ATT_BQ
ATT_HALF = 2 * CHUNK
ATT_BAND = (LEFT_CHUNKS + 1) * CHUNK + ATT_HALF - CHUNK
MIX_TM = 512
MIX_ROW_SPLIT = 2
POOL_HALO = 16
FFN_TM = 1024
FFN_TF = 512
FFN_ROW_SPLIT = 2
CONV_HALO = 8
NORM_ROWS = 16


def _rms_scale(x):
    return lax.rsqrt(jnp.mean(x * x, axis=-1, keepdims=True) + NORM_EPS)


def _row(vec):
    return vec.reshape(vec.shape[0], 1, vec.shape[1])


def _inproj_kernel(x_ref, g_ref, w_ref, z_ref, u_ref, h_ref):
    j = pl.program_id(1)

    @pl.when(j == 0)
    def _():
        x = x_ref[...]
        h_ref[...] = ((x * _rms_scale(x)) * g_ref[...]).astype(BF16)

    z = jnp.dot(h_ref[...], w_ref[...], preferred_element_type=F32)
    q_scale = jnp.where(j == 0, QK_SCALE, 1.0).astype(F32)
    z_ref[:, :ATTN_WIDTH] = (z[:, :ATTN_WIDTH] * q_scale).astype(BF16)
    z_ref[:, ATTN_WIDTH:] = z[:, ATTN_WIDTH:].astype(BF16)
    u_ref[...] = z[:, IN_TN - POOL_WIDTH:]


def _inproj(layer, x, g, w):
    t = x.shape[0]
    grid = (t // IN_TM, IN_WIDTH // IN_TN)
    return pl.pallas_call(
        _inproj_kernel,
        grid=grid,
        in_specs=[
            pl.BlockSpec((IN_TM, D_MODEL), lambda i, j: (i, 0)),
            pl.BlockSpec((None, 1, D_MODEL), lambda i, j: (layer, 0, 0)),
            pl.BlockSpec((None, D_MODEL, IN_TN), lambda i, j: (layer, 0, j)),
        ],
        out_specs=[
            pl.BlockSpec((IN_TM, IN_TN), lambda i, j: (i, j)),
            pl.BlockSpec((IN_TM, POOL_WIDTH), lambda i, j: (i, 0)),
        ],
        out_shape=[
            jax.ShapeDtypeStruct((t, IN_WIDTH), BF16),
            jax.ShapeDtypeStruct((t, POOL_WIDTH), F32),
        ],
        scratch_shapes=[pltpu.VMEM((IN_TM, D_MODEL), BF16)],
        compiler_params=pltpu.CompilerParams(
            dimension_semantics=("arbitrary", "arbitrary"),
            vmem_limit_bytes=VMEM_LIMIT_BYTES),
        name="inproj",
    )(x, g, w)


def _band_bias(table):
    n_heads = table.shape[0]
    band_keys = (LEFT_CHUNKS + 1) * CHUNK
    far = band_keys - REL_CLIP
    near = REL_CLIP + CHUNK - 1
    table = table.astype(F32)
    ext = jnp.concatenate(
        [jnp.broadcast_to(table[:, 2 * REL_CLIP:], (n_heads, far)),
         table[:, 2 * REL_CLIP - near:2 * REL_CLIP][:, ::-1]], axis=1)
    rows = jnp.stack([ext[:, CHUNK - 1 - qi:CHUNK - 1 - qi + band_keys] for qi in range(CHUNK)], axis=1)
    rows = rows * LOG2_E
    pad = ATT_BK - band_keys
    masked = jnp.full((n_heads, CHUNK, pad), MASK_VALUE, F32)
    blocks = [jnp.concatenate([masked[:, :, :c * CHUNK], rows, masked[:, :, :pad - c * CHUNK]], axis=2)
              for c in range(ATT_BQ // CHUNK)]
    bias = jnp.concatenate(blocks, axis=1)
    key_block = lax.broadcasted_iota(jnp.int32, (ATT_NKB, 1, 1, ATT_BK), 3) // ATT_BQ
    n_masked = lax.broadcasted_iota(jnp.int32, (ATT_NKB, 1, 1, ATT_BK), 0)
    return jnp.where(key_block < n_masked, F32(MASK_VALUE), bias[None])


def _attn_kernel(kvq_ref, bias_ref, o_ref, kt_ref, v_ref, p_ref, linv_ref):
    @pl.when(pl.program_id(0) == 0)
    def _():
        kt_ref[...] = jnp.zeros_like(kt_ref)
        v_ref[...] = jnp.zeros_like(v_ref)
        p_ref[...] = jnp.zeros_like(p_ref)
        linv_ref[...] = jnp.zeros_like(linv_ref)

    v_ref[0:ATT_BK, :] = v_ref[ATT_BQ:ATT_BK + ATT_BQ, :]
    v_ref[ATT_BK:ATT_BK + ATT_BQ, :] = kvq_ref[:, V_COL:V_COL + ATTN_WIDTH]
    kt_ref[:, 0:ATT_BK] = kt_ref[:, ATT_BQ:ATT_BK + ATT_BQ]
    kt_ref[:, ATT_BK:ATT_BK + ATT_BQ] = kvq_ref[:, K_COL:K_COL + ATTN_WIDTH].T

    for h in range(0, N_HEADS, 2):
        pair = p_ref[h:h + 2].reshape(2 * ATT_BQ, ATT_BK)
        o = jnp.dot(pair, v_ref[0:ATT_BK, h * HEAD_DIM:(h + 2) * HEAD_DIM], preferred_element_type=F32)
        for k in range(2):
            hs = slice((h + k) * HEAD_DIM, (h + k + 1) * HEAD_DIM)
            o_k = o[k * ATT_BQ:(k + 1) * ATT_BQ, k * HEAD_DIM:(k + 1) * HEAD_DIM]
            o_ref[:, hs] = (o_k * linv_ref[h + k]).astype(BF16)

    for h in range(N_HEADS):
        hs = slice(h * HEAD_DIM, (h + 1) * HEAD_DIM)
        qs = slice(Q_COL + h * HEAD_DIM, Q_COL + (h + 1) * HEAD_DIM)
        s = jnp.dot(kvq_ref[:, qs], kt_ref[hs, ATT_BQ:ATT_BK + ATT_BQ],
                    preferred_element_type=F32)
        for half in range(ATT_BQ // ATT_HALF):
            rows = slice(half * ATT_HALF, (half + 1) * ATT_HALF)
            cols = slice(half * ATT_HALF, half * ATT_HALF + ATT_BAND)
            sh = s[rows, cols] + bias_ref[h, rows, cols]
            m = jnp.max(sh, axis=-1, keepdims=True)
            p = jnp.exp2(sh - m)
            l = jnp.sum(p, axis=-1, keepdims=True)
            p_ref[h, rows, cols] = p.astype(BF16)
            linv_ref[h, rows, :] = jnp.broadcast_to(1.0 / l, (ATT_HALF, HEAD_DIM))


def _attention(z, bias, seq_len):
    t = z.shape[0]
    n_blocks = t // ATT_BQ
    blocks_per_seq = seq_len // ATT_BQ

    def cur_block(i):
        return jnp.minimum(i, n_blocks - 1)

    def bias_map(i):
        return (jnp.maximum(ATT_NKB - 1 - cur_block(i) % blocks_per_seq, 0), 0, 0, 0)

    return pl.pallas_call(
        _attn_kernel,
        grid=(n_blocks + 1,),
        in_specs=[pl.BlockSpec((ATT_BQ, 3 * ATTN_WIDTH), lambda i: (cur_block(i), 0)),
                  pl.BlockSpec((None, N_HEADS, ATT_BQ, ATT_BK), bias_map)],
        out_specs=pl.BlockSpec((ATT_BQ, ATTN_WIDTH), lambda i: (jnp.maximum(i - 1, 0), 0)),
        out_shape=jax.ShapeDtypeStruct((t, ATTN_WIDTH), BF16),
        scratch_shapes=[pltpu.VMEM((ATTN_WIDTH, ATT_BK + ATT_BQ), BF16),
                        pltpu.VMEM((ATT_BK + ATT_BQ, ATTN_WIDTH), BF16),
                        pltpu.VMEM((N_HEADS, ATT_BQ, ATT_BK), BF16),
                        pltpu.VMEM((N_HEADS, ATT_BQ, HEAD_DIM), F32)],
        compiler_params=pltpu.CompilerParams(
            dimension_semantics=("arbitrary",),
            vmem_limit_bytes=VMEM_LIMIT_BYTES),
        name="attention",
    )(z, bias)


def _mixout_kernel(tiles_per_seq, x_ref, a_ref, u_ref, uh_ref, pw_ref, ps_ref, woa_ref, wop_ref,
                   g_ref, o_ref, ebuf_ref, p_ref, m_ref):
    tm = MIX_TM
    sub = tm // MIX_ROW_SPLIT
    i_in_seq = pl.program_id(0) % tiles_per_seq

    ebuf_ref[0:POOL_HALO, :] = jnp.where(i_in_seq == 0, 0.0, uh_ref[...])
    ebuf_ref[POOL_HALO:, :] = u_ref[...]

    n_groups = len(POOL_WINDOWS)
    a_cols = D_MODEL // n_groups
    head_pos = i_in_seq * tm + lax.broadcasted_iota(jnp.int32, (POOL_HALO, POOL_GROUP), 0)

    def attn_cols(c):
        cols = slice(c * a_cols, (c + 1) * a_cols)
        m_ref[:, cols] = jnp.dot(a_ref[...], woa_ref[:, cols], preferred_element_type=F32)

    for g, w in enumerate(POOL_WINDOWS):
        if g > 0:
            attn_cols(g - 1)
        if g == n_groups - 1:
            attn_cols(g)

        cs = slice(g * POOL_GROUP, (g + 1) * POOL_GROUP)
        tok = ebuf_ref[POOL_HALO:, cs]
        wsum = tok
        for k in range(1, w):
            wsum = wsum + ebuf_ref[POOL_HALO - k:POOL_HALO - k + tm, cs]
        body = wsum * (1.0 / w) - tok
        head_count = jnp.minimum(head_pos + 1, w).astype(F32)
        head = wsum[:POOL_HALO] / head_count - tok[:POOL_HALO]
        pooled = jnp.concatenate([head, body[POOL_HALO:]], axis=0).astype(BF16)
        mixed = jnp.dot(pooled, pw_ref[g], preferred_element_type=F32)
        p_ref[:, cs] = (mixed * ps_ref[:, cs]).astype(BF16)

    for r in range(MIX_ROW_SPLIT):
        rows = slice(r * sub, (r + 1) * sub)
        m = m_ref[rows, :] + jnp.dot(p_ref[rows, :], wop_ref[...], preferred_element_type=F32)
        o_ref[rows, :] = x_ref[rows, :] + (m * _rms_scale(m)) * g_ref[...]


def _mixout(layer, x, a, u, pool_w, pool_scale, w_o, g, seq_len):
    t = x.shape[0]
    tm = MIX_TM
    tiles_per_seq = seq_len // tm
    halo_blocks = tm // POOL_HALO
    n_groups = len(POOL_WINDOWS)
    return pl.pallas_call(
        functools.partial(_mixout_kernel, tiles_per_seq),
        grid=(t // tm,),
        in_specs=[
            pl.BlockSpec((tm, D_MODEL), lambda i: (i, 0)),
            pl.BlockSpec((tm, ATTN_WIDTH), lambda i: (i, 0)),
            pl.BlockSpec((tm, POOL_WIDTH), lambda i: (i, 0)),
            pl.BlockSpec((POOL_HALO, POOL_WIDTH), lambda i: (jnp.maximum(i * halo_blocks - 1, 0), 0)),
            pl.BlockSpec((None, n_groups, POOL_GROUP, POOL_GROUP), lambda i: (layer, 0, 0, 0)),
            pl.BlockSpec((None, 1, POOL_WIDTH), lambda i: (layer, 0, 0)),
            pl.BlockSpec((None, ATTN_WIDTH, D_MODEL), lambda i: (layer, 0, 0)),
            pl.BlockSpec((None, POOL_WIDTH, D_MODEL), lambda i: (layer, 1, 0)),
            pl.BlockSpec((None, 1, D_MODEL), lambda i: (layer, 0, 0)),
        ],
        out_specs=pl.BlockSpec((tm, D_MODEL), lambda i: (i, 0)),
        out_shape=jax.ShapeDtypeStruct((t, D_MODEL), F32),
        scratch_shapes=[pltpu.VMEM((tm + POOL_HALO, POOL_WIDTH), F32),
                        pltpu.VMEM((tm, POOL_WIDTH), BF16),
                        pltpu.VMEM((tm, D_MODEL), F32)],
        compiler_params=pltpu.CompilerParams(
            dimension_semantics=("arbitrary",),
            vmem_limit_bytes=VMEM_LIMIT_BYTES),
        name="mixout",
    )(x, a, u, u, pool_w, pool_scale, w_o, w_o, g)


def _gelu_tanh(x):
    return 0.5 * x * (1.0 + jnp.tanh(math.sqrt(2.0 / math.pi) * (x + 0.044715 * (x * x * x))))


def _ffn_kernel(tiles_per_seq, x_ref, g_ref, wg_ref, wv_ref, cw_ref, cb_ref,
                wd_ref, pg_ref, o_ref, h_ref, ug_ref, uv_ref, carry_g_ref, carry_v_ref):
    tm = FFN_TM
    tf = FFN_TF
    i = pl.program_id(0)
    j = pl.program_id(1)
    first_in_seq = (i % tiles_per_seq) == 0
    y_ref = o_ref

    @pl.when(j == 0)
    def _():
        x = x_ref[...]
        h_ref[...] = ((x * _rms_scale(x)) * g_ref[...]).astype(BF16)

    @pl.when(jnp.logical_and(i == 0, j == 0))
    def _():
        y_ref[...] = jnp.zeros_like(y_ref)

    sub = tm // FFN_ROW_SPLIT

    def up(r, w_ref, buf_ref):
        rows = slice(r * sub, (r + 1) * sub)
        buf_ref[CONV_HALO + r * sub:CONV_HALO + (r + 1) * sub, :] = jnp.dot(
            h_ref[rows, :], w_ref[...], preferred_element_type=F32)

    gate_cols = pl.ds(pl.multiple_of(j * tf, tf), tf)
    val_cols = pl.ds(pl.multiple_of(D_FF + j * tf, tf), tf)

    def conv(r, cols, buf_ref):
        out = cb_ref[:, cols]
        for tap in range(CONV_WIDTH):
            start = CONV_HALO - (CONV_WIDTH - 1) + tap + r * sub
            out = out + buf_ref[start:start + sub, :] * cw_ref[tap:tap + 1, cols]
        return out

    ug_ref[0:CONV_HALO, :] = jnp.where(first_in_seq, 0.0, carry_g_ref[j])
    uv_ref[0:CONV_HALO, :] = jnp.where(first_in_seq, 0.0, carry_v_ref[j])
    for r in range(FFN_ROW_SPLIT):
        up(r, wg_ref, ug_ref)
        up(r, wv_ref, uv_ref)
    carry_g_ref[j] = ug_ref[tm:tm + CONV_HALO, :]
    carry_v_ref[j] = uv_ref[tm:tm + CONV_HALO, :]
    for r in range(FFN_ROW_SPLIT):
        rows = slice(r * sub, (r + 1) * sub)
        gate = conv(r, gate_cols, ug_ref)
        val = conv(r, val_cols, uv_ref)
        act = (_gelu_tanh(gate) * val).astype(BF16)
        acc = jnp.where(j == 0, 0.0, y_ref[rows, :])
        y_ref[rows, :] = acc + jnp.dot(act, wd_ref[...], preferred_element_type=F32)

    @pl.when(j == pl.num_programs(1) - 1)
    def _():
        for c in range(tm // NORM_ROWS):
            rows = slice(c * NORM_ROWS, (c + 1) * NORM_ROWS)
            yy = y_ref[rows, :]
            o_ref[rows, :] = x_ref[rows, :] + (yy * _rms_scale(yy)) * pg_ref[...]


def _ffn(layer, x, g, w_up, conv_w, conv_b, w_down, post_g, seq_len):
    t = x.shape[0]
    tm, tf = FFN_TM, FFN_TF
    n_f = D_FF // tf
    tiles_per_seq = seq_len // tm
    return pl.pallas_call(
        functools.partial(_ffn_kernel, tiles_per_seq),
        grid=(t // tm, n_f),
        in_specs=[
            pl.BlockSpec((tm, D_MODEL), lambda i, j: (i, 0)),
            pl.BlockSpec((None, 1, D_MODEL), lambda i, j: (layer, 0, 0)),
            pl.BlockSpec((None, D_MODEL, tf), lambda i, j: (layer, 0, j)),
            pl.BlockSpec((None, D_MODEL, tf), lambda i, j: (layer, 0, j + n_f)),
            pl.BlockSpec((None, CONV_WIDTH, 2 * D_FF), lambda i, j: (layer, 0, 0)),
            pl.BlockSpec((None, 1, 2 * D_FF), lambda i, j: (layer, 0, 0)),
            pl.BlockSpec((None, tf, D_MODEL), lambda i, j: (layer, j, 0)),
            pl.BlockSpec((None, 1, D_MODEL), lambda i, j: (layer, 0, 0)),
        ],
        out_specs=pl.BlockSpec((tm, D_MODEL), lambda i, j: (i, 0)),
        out_shape=jax.ShapeDtypeStruct((t, D_MODEL), F32),
        scratch_shapes=[
            pltpu.VMEM((tm, D_MODEL), BF16),
            pltpu.VMEM((tm + CONV_HALO, tf), F32),
            pltpu.VMEM((tm + CONV_HALO, tf), F32),
            pltpu.VMEM((n_f, CONV_HALO, tf), F32),
            pltpu.VMEM((n_f, CONV_HALO, tf), F32),
        ],
        compiler_params=pltpu.CompilerParams(
            dimension_semantics=("arbitrary", "arbitrary"),
            vmem_limit_bytes=VMEM_LIMIT_BYTES),
        name="ffn",
    )(x, g, w_up, w_up, conv_w, conv_b, w_down, post_g)


def kernel(x, pre_mix_g, w_in, rel_bias, pool_w, pool_scale, w_o, post_mix_g,
           pre_ffn_g, w_up, conv_w, conv_b, w_down, post_ffn_g):
    b, s, d = x.shape
    depth = w_in.shape[0]
    assert d == D_MODEL and w_in.shape[1:] == (D_MODEL, IN_WIDTH) and w_up.shape[1:] == (D_MODEL, 2 * D_FF)
    assert all(s % tile == 0 for tile in (IN_TM, ATT_BQ, MIX_TM, FFN_TM)) and s // ATT_BQ >= ATT_NKB
    w_in, pool_w, w_o, w_up, w_down = (w.astype(BF16) for w in (w_in, pool_w, w_o, w_up, w_down))
    pre_mix_g, pool_scale, post_mix_g, pre_ffn_g, conv_b, post_ffn_g = (
        _row(v) for v in (pre_mix_g, pool_scale, post_mix_g, pre_ffn_g, conv_b, post_ffn_g))
    xt = x.reshape(b * s, d)
    for l in range(depth):
        z, u = _inproj(l, xt, pre_mix_g, w_in)
        a = _attention(z, _band_bias(rel_bias[l]), s)
        xt = _mixout(l, xt, a, u, pool_w, pool_scale, w_o, post_mix_g, s)
        xt = _ffn(l, xt, pre_ffn_g, w_up, conv_w, conv_b, w_down, post_ffn_g, s)
    return xt.reshape(b, s, d)
```

```python
import functools
import math

import jax
import jax.numpy as jnp
from jax import lax
from jax.experimental import pallas as pl
from jax.experimental.pallas import tpu as pltpu

F32 = jnp.float32
BF16 = jnp.bfloat16

D_MODEL = 2048
CHUNK = 64
LEFT_CHUNKS = 8
ATTN_WIDTH = 1024
POOL_WIDTH = 1024
HEAD_DIM = 128
N_HEADS = ATTN_WIDTH // HEAD_DIM
REL_CLIP = 128
POOL_WINDOWS = (2, 4, 8, 16)
POOL_GROUP = POOL_WIDTH // len(POOL_WINDOWS)
IN_WIDTH = 3 * ATTN_WIDTH + POOL_WIDTH
D_FF = 5632
CONV_WIDTH = 3
NORM_EPS = 1e-6
MASK_VALUE = -1e30
LOG2_E = math.log2(math.e)
QK_SCALE = LOG2_E / math.sqrt(HEAD_DIM)
Q_COL, K_COL, V_COL = 0, ATTN_WIDTH, 2 * ATTN_WIDTH

V7X_VMEM_BYTES = 64 * 1024 * 1024
VMEM_RESERVE_BYTES = 1024 * 1024
VMEM_LIMIT_BYTES = V7X_VMEM_BYTES - VMEM_RESERVE_BYTES

IN_TM = 1024
IN_TN = 2048
ATT_BQ = 4 * CHUNK
ATT_NKB = LEFT_CHUNKS * CHUNK // ATT_BQ + 1
ATT_BK = ATT_NKB * ATT_BQ
ATT_HALF = 2 * CHUNK
ATT_BAND = (LEFT_CHUNKS + 1) * CHUNK + ATT_HALF - CHUNK
MIX_TM = 512
MIX_ROW_SPLIT = 2
POOL_HALO = 16
POOL_PAD = 8
FFN_TM = 1024
FFN_TF = 512
FFN_ROW_SPLIT = 2
CONV_HALO = 8
NORM_ROWS = 16


def _rms_scale(x):
    return lax.rsqrt(jnp.mean(x * x, axis=-1, keepdims=True) + NORM_EPS)


def _row(vec):
    return vec.reshape(vec.shape[0], 1, vec.shape[1])


def _inproj_kernel(x_ref, g_ref, w_ref, z_ref, u_ref, h_ref):
    j = pl.program_id(1)

    @pl.when(j == 0)
    def _():
        x = x_ref[...]
        h_ref[...] = ((x * _rms_scale(x)) * g_ref[...]).astype(BF16)

    z = jnp.dot(h_ref[...], w_ref[...], preferred_element_type=F32)
    q_scale = jnp.where(j == 0, QK_SCALE, 1.0).astype(F32)
    z_ref[:, :ATTN_WIDTH] = (z[:, :ATTN_WIDTH] * q_scale).astype(BF16)
    z_ref[:, ATTN_WIDTH:] = z[:, ATTN_WIDTH:].astype(BF16)
    u_ref[...] = z[:, IN_TN - POOL_WIDTH:]


def _inproj(layer, x, g, w):
    t = x.shape[0]
    grid = (t // IN_TM, IN_WIDTH // IN_TN)
    return pl.pallas_call(
        _inproj_kernel,
        grid=grid,
        in_specs=[
            pl.BlockSpec((IN_TM, D_MODEL), lambda i, j: (i, 0)),
            pl.BlockSpec((None, 1, D_MODEL), lambda i, j: (layer, 0, 0)),
            pl.BlockSpec((None, D_MODEL, IN_TN), lambda i, j: (layer, 0, j)),
        ],
        out_specs=[
            pl.BlockSpec((IN_TM, IN_TN), lambda i, j: (i, j)),
            pl.BlockSpec((IN_TM, POOL_WIDTH), lambda i, j: (i, 0)),
        ],
        out_shape=[
            jax.ShapeDtypeStruct((t, IN_WIDTH), BF16),
            jax.ShapeDtypeStruct((t, POOL_WIDTH), F32),
        ],
        scratch_shapes=[pltpu.VMEM((IN_TM, D_MODEL), BF16)],
        compiler_params=pltpu.CompilerParams(
            dimension_semantics=("arbitrary", "arbitrary"),
            vmem_limit_bytes=VMEM_LIMIT_BYTES),
        name="inproj",
    )(x, g, w)


def _band_bias(table):
    n_heads = table.shape[0]
    band_keys = (LEFT_CHUNKS + 1) * CHUNK
    far = band_keys - REL_CLIP
    near = REL_CLIP + CHUNK - 1
    table = table.astype(F32)
    ext = jnp.concatenate(
        [jnp.broadcast_to(table[:, 2 * REL_CLIP:], (n_heads, far)),
         table[:, 2 * REL_CLIP - near:2 * REL_CLIP][:, ::-1]], axis=1)
    rows = jnp.stack([ext[:, CHUNK - 1 - qi:CHUNK - 1 - qi + band_keys] for qi in range(CHUNK)], axis=1)
    rows = rows * LOG2_E
    pad = ATT_BK - band_keys
    masked = jnp.full((n_heads, CHUNK, pad), MASK_VALUE, F32)
    blocks = [jnp.concatenate([masked[:, :, :c * CHUNK], rows, masked[:, :, :pad - c * CHUNK]], axis=2)
              for c in range(ATT_BQ // CHUNK)]
    bias = jnp.concatenate(blocks, axis=1)
    key_block = lax.broadcasted_iota(jnp.int32, (ATT_NKB, 1, 1, ATT_BK), 3) // ATT_BQ
    n_masked = lax.broadcasted_iota(jnp.int32, (ATT_NKB, 1, 1, ATT_BK), 0)
    return jnp.where(key_block < n_masked, F32(MASK_VALUE), bias[None])


def _attn_kernel(kvq_ref, bias_ref, o_ref, kt_ref, v_ref, p_ref, linv_ref):
    @pl.when(pl.program_id(0) == 0)
    def _():
        kt_ref[...] = jnp.zeros_like(kt_ref)
        v_ref[...] = jnp.zeros_like(v_ref)
        p_ref[...] = jnp.zeros_like(p_ref)
        linv_ref[...] = jnp.zeros_like(linv_ref)

    v_ref[0:ATT_BK, :] = v_ref[ATT_BQ:ATT_BK + ATT_BQ, :]
    v_ref[ATT_BK:ATT_BK + ATT_BQ, :] = kvq_ref[:, V_COL:V_COL + ATTN_WIDTH]
    kt_ref[:, 0:ATT_BK] = kt_ref[:, ATT_BQ:ATT_BK + ATT_BQ]
    kt_ref[:, ATT_BK:ATT_BK + ATT_BQ] = kvq_ref[:, K_COL:K_COL + ATTN_WIDTH].T

    for h in range(0, N_HEADS, 2):
        pair = p_ref[h:h + 2].reshape(2 * ATT_BQ, ATT_BK)
        o = jnp.dot(pair, v_ref[0:ATT_BK, h * HEAD_DIM:(h + 2) * HEAD_DIM], preferred_element_type=F32)
        for k in range(2):
            hs = slice((h + k) * HEAD_DIM, (h + k + 1) * HEAD_DIM)
            o_k = o[k * ATT_BQ:(k + 1) * ATT_BQ, k * HEAD_DIM:(k + 1) * HEAD_DIM]
            o_ref[:, hs] = (o_k * linv_ref[h + k]).astype(BF16)

    for h in range(N_HEADS):
        hs = slice(h * HEAD_DIM, (h + 1) * HEAD_DIM)
        qs = slice(Q_COL + h * HEAD_DIM, Q_COL + (h + 1) * HEAD_DIM)
        s = jnp.dot(kvq_ref[:, qs], kt_ref[hs, ATT_BQ:ATT_BK + ATT_BQ],
                    preferred_element_type=F32)
        for half in range(ATT_BQ // ATT_HALF):
            rows = slice(half * ATT_HALF, (half + 1) * ATT_HALF)
            cols = slice(half * ATT_HALF, half * ATT_HALF + ATT_BAND)
            sh = s[rows, cols] + bias_ref[h, rows, cols]
            m = jnp.max(sh, axis=-1, keepdims=True)
            p = jnp.exp2(sh - m)
            l = jnp.sum(p, axis=-1, keepdims=True)
            p_ref[h, rows, cols] = p.astype(BF16)
            linv_ref[h, rows, :] = jnp.broadcast_to(1.0 / l, (ATT_HALF, HEAD_DIM))


def _attention(z, bias, seq_len):
    t = z.shape[0]
    n_blocks = t // ATT_BQ
    blocks_per_seq = seq_len // ATT_BQ

    def cur_block(i):
        return jnp.minimum(i, n_blocks - 1)

    def bias_map(i):
        return (jnp.maximum(ATT_NKB - 1 - cur_block(i) % blocks_per_seq, 0), 0, 0, 0)

    return pl.pallas_call(
        _attn_kernel,
        grid=(n_blocks + 1,),
        in_specs=[pl.BlockSpec((ATT_BQ, 3 * ATTN_WIDTH), lambda i: (cur_block(i), 0)),
                  pl.BlockSpec((None, N_HEADS, ATT_BQ, ATT_BK), bias_map)],
        out_specs=pl.BlockSpec((ATT_BQ, ATTN_WIDTH), lambda i: (jnp.maximum(i - 1, 0), 0)),
        out_shape=jax.ShapeDtypeStruct((t, ATTN_WIDTH), BF16),
        scratch_shapes=[pltpu.VMEM((ATTN_WIDTH, ATT_BK + ATT_BQ), BF16),
                        pltpu.VMEM((ATT_BK + ATT_BQ, ATTN_WIDTH), BF16),
                        pltpu.VMEM((N_HEADS, ATT_BQ, ATT_BK), BF16),
                        pltpu.VMEM((N_HEADS, ATT_BQ, HEAD_DIM), F32)],
        compiler_params=pltpu.CompilerParams(
            dimension_semantics=("arbitrary",),
            vmem_limit_bytes=VMEM_LIMIT_BYTES),
        name="attention",
    )(z, bias)


def _mixout_kernel(tiles_per_seq, x_ref, a_ref, u_ref, uh_ref, pw_ref, ps_ref, woa_ref, wop_ref,
                   g_ref, o_ref, ebuf_ref, p_ref, m_ref, sa_ref, sb_ref):
    tm = MIX_TM
    sub = tm // MIX_ROW_SPLIT
    i_in_seq = pl.program_id(0) % tiles_per_seq
    ext = POOL_HALO + tm
    own = POOL_PAD + POOL_HALO

    @pl.when(pl.program_id(0) == 0)
    def _():
        ebuf_ref[0:POOL_PAD, :] = jnp.zeros((POOL_PAD, POOL_WIDTH), F32)
        sa_ref[0:POOL_PAD, :] = jnp.zeros((POOL_PAD, POOL_GROUP), F32)
        sb_ref[0:POOL_PAD, :] = jnp.zeros((POOL_PAD, POOL_GROUP), F32)

    ebuf_ref[POOL_PAD:own, :] = jnp.where(i_in_seq == 0, 0.0, uh_ref[...])
    ebuf_ref[own:, :] = u_ref[...]

    def window_sum(w, cs):
        src, cols, span, spare = ebuf_ref, cs, 1, [sa_ref, sb_ref]
        while 2 * span < w:
            dst = spare.pop(0)
            dst[POOL_PAD:POOL_PAD + ext, :] = (src[POOL_PAD:POOL_PAD + ext, cols]
                                               + src[POOL_PAD - span:POOL_PAD - span + ext, cols])
            if src is not ebuf_ref:
                spare.append(src)
            src, cols, span = dst, slice(None), 2 * span
        return src[own:own + tm, cols] + src[own - span:own - span + tm, cols]

    n_groups = len(POOL_WINDOWS)
    a_cols = D_MODEL // n_groups
    head_pos = i_in_seq * tm + lax.broadcasted_iota(jnp.int32, (POOL_HALO, POOL_GROUP), 0)

    def attn_cols(c):
        cols = slice(c * a_cols, (c + 1) * a_cols)
        m_ref[:, cols] = jnp.dot(a_ref[...], woa_ref[:, cols], preferred_element_type=F32)

    for g, w in enumerate(POOL_WINDOWS):
        if g > 0:
            attn_cols(g - 1)
        if g == n_groups - 1:
            attn_cols(g)

        cs = slice(g * POOL_GROUP, (g + 1) * POOL_GROUP)
        tok = ebuf_ref[own:own + tm, cs]
        wsum = window_sum(w, cs)
        body = wsum * (1.0 / w) - tok
        head_count = jnp.minimum(head_pos + 1, w).astype(F32)
        head = wsum[:POOL_HALO] / head_count - tok[:POOL_HALO]
        pooled = jnp.concatenate([head, body[POOL_HALO:]], axis=0).astype(BF16)
        mixed = jnp.dot(pooled, pw_ref[g], preferred_element_type=F32)
        p_ref[:, cs] = (mixed * ps_ref[:, cs]).astype(BF16)

    for r in range(MIX_ROW_SPLIT):
        rows = slice(r * sub, (r + 1) * sub)
        m = m_ref[rows, :] + jnp.dot(p_ref[rows, :], wop_ref[...], preferred_element_type=F32)
        o_ref[rows, :] = x_ref[rows, :] + (m * _rms_scale(m)) * g_ref[...]


def _mixout(layer, x, a, u, pool_w, pool_scale, w_o, g, seq_len):
    t = x.shape[0]
    tm = MIX_TM
    tiles_per_seq = seq_len // tm
    halo_blocks = tm // POOL_HALO
    n_groups = len(POOL_WINDOWS)
    return pl.pallas_call(
        functools.partial(_mixout_kernel, tiles_per_seq),
        grid=(t // tm,),
        in_specs=[
            pl.BlockSpec((tm, D_MODEL), lambda i: (i, 0)),
            pl.BlockSpec((tm, ATTN_WIDTH), lambda i: (i, 0)),
            pl.BlockSpec((tm, POOL_WIDTH), lambda i: (i, 0)),
            pl.BlockSpec((POOL_HALO, POOL_WIDTH), lambda i: (jnp.maximum(i * halo_blocks - 1, 0), 0)),
            pl.BlockSpec((None, n_groups, POOL_GROUP, POOL_GROUP), lambda i: (layer, 0, 0, 0)),
            pl.BlockSpec((None, 1, POOL_WIDTH), lambda i: (layer, 0, 0)),
            pl.BlockSpec((None, ATTN_WIDTH, D_MODEL), lambda i: (layer, 0, 0)),
            pl.BlockSpec((None, POOL_WIDTH, D_MODEL), lambda i: (layer, 1, 0)),
            pl.BlockSpec((None, 1, D_MODEL), lambda i: (layer, 0, 0)),
        ],
        out_specs=pl.BlockSpec((tm, D_MODEL), lambda i: (i, 0)),
        out_shape=jax.ShapeDtypeStruct((t, D_MODEL), F32),
        scratch_shapes=[pltpu.VMEM((POOL_PAD + POOL_HALO + tm, POOL_WIDTH), F32),
                        pltpu.VMEM((tm, POOL_WIDTH), BF16),
                        pltpu.VMEM((tm, D_MODEL), F32),
                        pltpu.VMEM((POOL_PAD + POOL_HALO + tm, POOL_GROUP), F32),
                        pltpu.VMEM((POOL_PAD + POOL_HALO + tm, POOL_GROUP), F32)],
        compiler_params=pltpu.CompilerParams(
            dimension_semantics=("arbitrary",),
            vmem_limit_bytes=VMEM_LIMIT_BYTES),
        name="mixout",
    )(x, a, u, u, pool_w, pool_scale, w_o, w_o, g)


def _gelu_tanh(x):
    return 0.5 * x * (1.0 + jnp.tanh(math.sqrt(2.0 / math.pi) * (x + 0.044715 * (x * x * x))))


def _ffn_kernel(tiles_per_seq, x_ref, g_ref, wg_ref, wv_ref, cw_ref, cb_ref,
                wd_ref, pg_ref, o_ref, h_ref, ug_ref, uv_ref, carry_g_ref, carry_v_ref):
    tm = FFN_TM
    tf = FFN_TF
    i = pl.program_id(0)
    j = pl.program_id(1)
    first_in_seq = (i % tiles_per_seq) == 0
    y_ref = o_ref

    @pl.when(j == 0)
    def _():
        x = x_ref[...]
        h_ref[...] = ((x * _rms_scale(x)) * g_ref[...]).astype(BF16)

    @pl.when(jnp.logical_and(i == 0, j == 0))
    def _():
        y_ref[...] = jnp.zeros_like(y_ref)

    sub = tm // FFN_ROW_SPLIT

    def up(r, w_ref, buf_ref):
        rows = slice(r * sub, (r + 1) * sub)
        buf_ref[CONV_HALO + r * sub:CONV_HALO + (r + 1) * sub, :] = jnp.dot(
            h_ref[rows, :], w_ref[...], preferred_element_type=F32)

    gate_cols = pl.ds(pl.multiple_of(j * tf, tf), tf)
    val_cols = pl.ds(pl.multiple_of(D_FF + j * tf, tf), tf)

    def conv(r, cols, buf_ref):
        out = cb_ref[:, cols]
        for tap in range(CONV_WIDTH):
            start = CONV_HALO - (CONV_WIDTH - 1) + tap + r * sub
            out = out + buf_ref[start:start + sub, :] * cw_ref[tap:tap + 1, cols]
        return out

    ug_ref[0:CONV_HALO, :] = jnp.where(first_in_seq, 0.0, carry_g_ref[j])
    uv_ref[0:CONV_HALO, :] = jnp.where(first_in_seq, 0.0, carry_v_ref[j])
    for r in range(FFN_ROW_SPLIT):
        up(r, wg_ref, ug_ref)
        up(r, wv_ref, uv_ref)
    carry_g_ref[j] = ug_ref[tm:tm + CONV_HALO, :]
    carry_v_ref[j] = uv_ref[tm:tm + CONV_HALO, :]
    for r in range(FFN_ROW_SPLIT):
        rows = slice(r * sub, (r + 1) * sub)
        gate = conv(r, gate_cols, ug_ref)
        val = conv(r, val_cols, uv_ref)
        act = (_gelu_tanh(gate) * val).astype(BF16)
        acc = jnp.where(j == 0, 0.0, y_ref[rows, :])
        y_ref[rows, :] = acc + jnp.dot(act, wd_ref[...], preferred_element_type=F32)

    @pl.when(j == pl.num_programs(1) - 1)
    def _():
        for c in range(tm // NORM_ROWS):
            rows = slice(c * NORM_ROWS, (c + 1) * NORM_ROWS)
            yy = y_ref[rows, :]
            o_ref[rows, :] = x_ref[rows, :] + (yy * _rms_scale(yy)) * pg_ref[...]


def _ffn(layer, x, g, w_up, conv_w, conv_b, w_down, post_g, seq_len):
    t = x.shape[0]
    tm, tf = FFN_TM, FFN_TF
    n_f = D_FF // tf
    tiles_per_seq = seq_len // tm
    return pl.pallas_call(
        functools.partial(_ffn_kernel, tiles_per_seq),
        grid=(t // tm, n_f),
        in_specs=[
            pl.BlockSpec((tm, D_MODEL), lambda i, j: (i, 0)),
            pl.BlockSpec((None, 1, D_MODEL), lambda i, j: (layer, 0, 0)),
            pl.BlockSpec((None, D_MODEL, tf), lambda i, j: (layer, 0, j)),
            pl.BlockSpec((None, D_MODEL, tf), lambda i, j: (layer, 0, j + n_f)),
            pl.BlockSpec((None, CONV_WIDTH, 2 * D_FF), lambda i, j: (layer, 0, 0)),
            pl.BlockSpec((None, 1, 2 * D_FF), lambda i, j: (layer, 0, 0)),
            pl.BlockSpec((None, tf, D_MODEL), lambda i, j: (layer, j, 0)),
            pl.BlockSpec((None, 1, D_MODEL), lambda i, j: (layer, 0, 0)),
        ],
        out_specs=pl.BlockSpec((tm, D_MODEL), lambda i, j: (i, 0)),
        out_shape=jax.ShapeDtypeStruct((t, D_MODEL), F32),
        scratch_shapes=[
            pltpu.VMEM((tm, D_MODEL), BF16),
            pltpu.VMEM((tm + CONV_HALO, tf), F32),
            pltpu.VMEM((tm + CONV_HALO, tf), F32),
            pltpu.VMEM((n_f, CONV_HALO, tf), F32),
            pltpu.VMEM((n_f, CONV_HALO, tf), F32),
        ],
        compiler_params=pltpu.CompilerParams(
            dimension_semantics=("arbitrary", "arbitrary"),
            vmem_limit_bytes=VMEM_LIMIT_BYTES),
        name="ffn",
    )(x, g, w_up, w_up, conv_w, conv_b, w_down, post_g)


def kernel(x, pre_mix_g, w_in, rel_bias, pool_w, pool_scale, w_o, post_mix_g,
           pre_ffn_g, w_up, conv_w, conv_b, w_down, post_ffn_g):
    b, s, d = x.shape
    depth = w_in.shape[0]
    assert d == D_MODEL and w_in.shape[1:] == (D_MODEL, IN_WIDTH) and w_up.shape[1:] == (D_MODEL, 2 * D_FF)
    assert all(s % tile == 0 for tile in (IN_TM, ATT_BQ, MIX_TM, FFN_TM)) and s // ATT_BQ >= ATT_NKB
    w_in, pool_w, w_o, w_up, w_down = (w.astype(BF16) for w in (w_in, pool_w, w_o, w_up, w_down))
    pre_mix_g, pool_scale, post_mix_g, pre_ffn_g, conv_b, post_ffn_g = (
        _row(v) for v in (pre_mix_g, pool_scale, post_mix_g, pre_ffn_g, conv_b, post_ffn_g))
    xt = x.reshape(b * s, d)
    for l in range(depth):
        z, u = _inproj(l, xt, pre_mix_g, w_in)
        a = _attention(z, _band_bias(rel_bias[l]), s)
        xt = _mixout(l, xt, a, u, pool_w, pool_scale, w_o, post_mix_g, s)
        xt = _ffn(l, xt, pre_ffn_g, w_up, conv_w, conv_b, w_down, post_ffn_g, s)
    return xt.reshape(b, s, d)
```

```python
import functools
import math

import jax
import jax.numpy as jnp
from jax import lax
from jax.experimental import pallas as pl
from jax.experimental.pallas import tpu as pltpu

F32 = jnp.float32
BF16 = jnp.bfloat16

D_MODEL = 2048
CHUNK = 64
LEFT_CHUNKS = 8
ATTN_WIDTH = 1024
POOL_WIDTH = 1024
HEAD_DIM = 128
N_HEADS = ATTN_WIDTH // HEAD_DIM
REL_CLIP = 128
POOL_WINDOWS = (2, 4, 8, 16)
POOL_GROUP = POOL_WIDTH // len(POOL_WINDOWS)
IN_WIDTH = 3 * ATTN_WIDTH + POOL_WIDTH
D_FF = 5632
CONV_WIDTH = 3
NORM_EPS = 1e-6
MASK_VALUE = -1e30
LOG2_E = math.log2(math.e)
QK_SCALE = LOG2_E / math.sqrt(HEAD_DIM)
Q_COL, K_COL, V_COL = 0, ATTN_WIDTH, 2 * ATTN_WIDTH

V7X_VMEM_BYTES = 64 * 1024 * 1024
VMEM_RESERVE_BYTES = 1024 * 1024
VMEM_LIMIT_BYTES = V7X_VMEM_BYTES - VMEM_RESERVE_BYTES

IN_TM = 1024
IN_TN = 2048
ATT_BQ = 4 * CHUNK
ATT_NKB = LEFT_CHUNKS * CHUNK // ATT_BQ + 1
ATT_BK = ATT_NKB * ATT_BQ
ATT_HALF = 2 * CHUNK
ATT_BAND = (LEFT_CHUNKS + 1) * CHUNK + ATT_HALF - CHUNK
MIX_TM = 512
MIX_ROW_SPLIT = 2
POOL_HALO = 16
POOL_PAD = 8
FFN_TM = 1024
FFN_TF = 512
FFN_ROW_SPLIT = 2
CONV_HALO = 8
NORM_ROWS = 16


def _rms_scale(x):
    return lax.rsqrt(jnp.mean(x * x, axis=-1, keepdims=True) + NORM_EPS)


def _row(vec):
    return vec.reshape(vec.shape[0], 1, vec.shape[1])


def _inproj_kernel(x_ref, g_ref, w_ref, z_ref, u_ref, h_ref):
    j = pl.program_id(1)
    half = IN_TM // 2

    def normed(rows):
        x = x_ref[rows, :]
        return ((x * _rms_scale(x)) * g_ref[...]).astype(BF16)

    @pl.when(j == 0)
    def _():
        h_ref[0:half, :] = normed(slice(0, half))

    h_ref[half:, :] = normed(slice(half, IN_TM))
    q_scale = jnp.where(j == 0, QK_SCALE, 1.0).astype(F32)
    for rows in (slice(0, half), slice(half, IN_TM)):
        z = jnp.dot(h_ref[rows, :], w_ref[...], preferred_element_type=F32)
        z_ref[rows, :ATTN_WIDTH] = (z[:, :ATTN_WIDTH] * q_scale).astype(BF16)
        z_ref[rows, ATTN_WIDTH:] = z[:, ATTN_WIDTH:].astype(BF16)
        u_ref[rows, :] = z[:, IN_TN - POOL_WIDTH:]


def _inproj(layer, x, g, w):
    t = x.shape[0]
    grid = (t // IN_TM, IN_WIDTH // IN_TN)
    return pl.pallas_call(
        _inproj_kernel,
        grid=grid,
        in_specs=[
            pl.BlockSpec((IN_TM, D_MODEL), lambda i, j: (i, 0)),
            pl.BlockSpec((None, 1, D_MODEL), lambda i, j: (layer, 0, 0)),
            pl.BlockSpec((None, D_MODEL, IN_TN), lambda i, j: (layer, 0, j)),
        ],
        out_specs=[
            pl.BlockSpec((IN_TM, IN_TN), lambda i, j: (i, j)),
            pl.BlockSpec((IN_TM, POOL_WIDTH), lambda i, j: (i, 0)),
        ],
        out_shape=[
            jax.ShapeDtypeStruct((t, IN_WIDTH), BF16),
            jax.ShapeDtypeStruct((t, POOL_WIDTH), F32),
        ],
        scratch_shapes=[pltpu.VMEM((IN_TM, D_MODEL), BF16)],
        compiler_params=pltpu.CompilerParams(
            dimension_semantics=("arbitrary", "arbitrary"),
            vmem_limit_bytes=VMEM_LIMIT_BYTES),
        name="inproj",
    )(x, g, w)


def _band_bias(table):
    n_heads = table.shape[0]
    band_keys = (LEFT_CHUNKS + 1) * CHUNK
    far = band_keys - REL_CLIP
    near = REL_CLIP + CHUNK - 1
    table = table.astype(F32)
    ext = jnp.concatenate(
        [jnp.broadcast_to(table[:, 2 * REL_CLIP:], (n_heads, far)),
         table[:, 2 * REL_CLIP - near:2 * REL_CLIP][:, ::-1]], axis=1)
    rows = jnp.stack([ext[:, CHUNK - 1 - qi:CHUNK - 1 - qi + band_keys] for qi in range(CHUNK)], axis=1)
    rows = rows * LOG2_E
    pad = ATT_BK - band_keys
    masked = jnp.full((n_heads, CHUNK, pad), MASK_VALUE, F32)
    blocks = [jnp.concatenate([masked[:, :, :c * CHUNK], rows, masked[:, :, :pad - c * CHUNK]], axis=2)
              for c in range(ATT_BQ // CHUNK)]
    bias = jnp.concatenate(blocks, axis=1)
    key_block = lax.broadcasted_iota(jnp.int32, (ATT_NKB, 1, 1, ATT_BK), 3) // ATT_BQ
    n_masked = lax.broadcasted_iota(jnp.int32, (ATT_NKB, 1, 1, ATT_BK), 0)
    return jnp.where(key_block < n_masked, F32(MASK_VALUE), bias[None])


def _attn_kernel(kvq_ref, bias_ref, o_ref, kt_ref, v_ref, p_ref, linv_ref):
    @pl.when(pl.program_id(0) == 0)
    def _():
        kt_ref[...] = jnp.zeros_like(kt_ref)
        v_ref[...] = jnp.zeros_like(v_ref)
        p_ref[...] = jnp.zeros_like(p_ref)
        linv_ref[...] = jnp.zeros_like(linv_ref)

    v_ref[0:ATT_BK, :] = v_ref[ATT_BQ:ATT_BK + ATT_BQ, :]
    v_ref[ATT_BK:ATT_BK + ATT_BQ, :] = kvq_ref[:, V_COL:V_COL + ATTN_WIDTH]
    kt_ref[:, 0:ATT_BK] = kt_ref[:, ATT_BQ:ATT_BK + ATT_BQ]
    kt_ref[:, ATT_BK:ATT_BK + ATT_BQ] = kvq_ref[:, K_COL:K_COL + ATTN_WIDTH].T

    for h in range(0, N_HEADS, 2):
        pair = p_ref[h:h + 2].reshape(2 * ATT_BQ, ATT_BK)
        o = jnp.dot(pair, v_ref[0:ATT_BK, h * HEAD_DIM:(h + 2) * HEAD_DIM], preferred_element_type=F32)
        for k in range(2):
            hs = slice((h + k) * HEAD_DIM, (h + k + 1) * HEAD_DIM)
            o_k = o[k * ATT_BQ:(k + 1) * ATT_BQ, k * HEAD_DIM:(k + 1) * HEAD_DIM]
            o_ref[:, hs] = (o_k * linv_ref[h + k]).astype(BF16)

    for h in range(N_HEADS):
        hs = slice(h * HEAD_DIM, (h + 1) * HEAD_DIM)
        qs = slice(Q_COL + h * HEAD_DIM, Q_COL + (h + 1) * HEAD_DIM)
        s = jnp.dot(kvq_ref[:, qs], kt_ref[hs, ATT_BQ:ATT_BK + ATT_BQ],
                    preferred_element_type=F32)
        for half in range(ATT_BQ // ATT_HALF):
            rows = slice(half * ATT_HALF, (half + 1) * ATT_HALF)
            cols = slice(half * ATT_HALF, half * ATT_HALF + ATT_BAND)
            sh = s[rows, cols] + bias_ref[h, rows, cols]
            m = jnp.max(sh, axis=-1, keepdims=True)
            p = jnp.exp2(sh - m)
            l = jnp.sum(p, axis=-1, keepdims=True)
            p_ref[h, rows, cols] = p.astype(BF16)
            linv_ref[h, rows, :] = jnp.broadcast_to(1.0 / l, (ATT_HALF, HEAD_DIM))


def _attention(z, bias, seq_len):
    t = z.shape[0]
    n_blocks = t // ATT_BQ
    blocks_per_seq = seq_len // ATT_BQ

    def cur_block(i):
        return jnp.minimum(i, n_blocks - 1)

    def bias_map(i):
        return (jnp.maximum(ATT_NKB - 1 - cur_block(i) % blocks_per_seq, 0), 0, 0, 0)

    return pl.pallas_call(
        _attn_kernel,
        grid=(n_blocks + 1,),
        in_specs=[pl.BlockSpec((ATT_BQ, 3 * ATTN_WIDTH), lambda i: (cur_block(i), 0)),
                  pl.BlockSpec((None, N_HEADS, ATT_BQ, ATT_BK), bias_map)],
        out_specs=pl.BlockSpec((ATT_BQ, ATTN_WIDTH), lambda i: (jnp.maximum(i - 1, 0), 0)),
        out_shape=jax.ShapeDtypeStruct((t, ATTN_WIDTH), BF16),
        scratch_shapes=[pltpu.VMEM((ATTN_WIDTH, ATT_BK + ATT_BQ), BF16),
                        pltpu.VMEM((ATT_BK + ATT_BQ, ATTN_WIDTH), BF16),
                        pltpu.VMEM((N_HEADS, ATT_BQ, ATT_BK), BF16),
                        pltpu.VMEM((N_HEADS, ATT_BQ, HEAD_DIM), F32)],
        compiler_params=pltpu.CompilerParams(
            dimension_semantics=("arbitrary",),
            vmem_limit_bytes=VMEM_LIMIT_BYTES),
        name="attention",
    )(z, bias)


def _mixout_kernel(tiles_per_seq, x_ref, a_ref, u_ref, uh_ref, pw_ref, ps_ref, woa_ref, wop_ref,
                   g_ref, o_ref, ebuf_ref, p_ref, m_ref, sa_ref, sb_ref):
    tm = MIX_TM
    sub = tm // MIX_ROW_SPLIT
    i_in_seq = pl.program_id(0) % tiles_per_seq
    ext = POOL_HALO + tm
    own = POOL_PAD + POOL_HALO

    @pl.when(pl.program_id(0) == 0)
    def _():
        ebuf_ref[0:POOL_PAD, :] = jnp.zeros((POOL_PAD, POOL_WIDTH), F32)
        sa_ref[0:POOL_PAD, :] = jnp.zeros((POOL_PAD, POOL_GROUP), F32)
        sb_ref[0:POOL_PAD, :] = jnp.zeros((POOL_PAD, POOL_GROUP), F32)

    ebuf_ref[POOL_PAD:own, :] = jnp.where(i_in_seq == 0, 0.0, uh_ref[...])
    ebuf_ref[own:, :] = u_ref[...]

    def window_sum(w, cs):
        src, cols, span, spare = ebuf_ref, cs, 1, [sa_ref, sb_ref]
        while 2 * span < w:
            dst = spare.pop(0)
            dst[POOL_PAD:POOL_PAD + ext, :] = (src[POOL_PAD:POOL_PAD + ext, cols]
                                               + src[POOL_PAD - span:POOL_PAD - span + ext, cols])
            if src is not ebuf_ref:
                spare.append(src)
            src, cols, span = dst, slice(None), 2 * span
        return src[own:own + tm, cols] + src[own - span:own - span + tm, cols]

    n_groups = len(POOL_WINDOWS)
    a_cols = D_MODEL // n_groups
    head_pos = i_in_seq * tm + lax.broadcasted_iota(jnp.int32, (POOL_HALO, POOL_GROUP), 0)

    def attn_cols(c):
        cols = slice(c * a_cols, (c + 1) * a_cols)
        m_ref[:, cols] = jnp.dot(a_ref[...], woa_ref[:, cols], preferred_element_type=F32)

    for g, w in enumerate(POOL_WINDOWS):
        if g > 0:
            attn_cols(g - 1)
        if g == n_groups - 1:
            attn_cols(g)

        cs = slice(g * POOL_GROUP, (g + 1) * POOL_GROUP)
        tok = ebuf_ref[own:own + tm, cs]
        wsum = window_sum(w, cs)
        body = wsum * (1.0 / w) - tok
        head_count = jnp.minimum(head_pos + 1, w).astype(F32)
        head = wsum[:POOL_HALO] / head_count - tok[:POOL_HALO]
        pooled = jnp.concatenate([head, body[POOL_HALO:]], axis=0).astype(BF16)
        mixed = jnp.dot(pooled, pw_ref[g], preferred_element_type=F32)
        p_ref[:, cs] = (mixed * ps_ref[:, cs]).astype(BF16)

    for r in range(MIX_ROW_SPLIT):
        rows = slice(r * sub, (r + 1) * sub)
        m = m_ref[rows, :] + jnp.dot(p_ref[rows, :], wop_ref[...], preferred_element_type=F32)
        o_ref[rows, :] = x_ref[rows, :] + (m * _rms_scale(m)) * g_ref[...]


def _mixout(layer, x, a, u, pool_w, pool_scale, w_o, g, seq_len):
    t = x.shape[0]
    tm = MIX_TM
    tiles_per_seq = seq_len // tm
    halo_blocks = tm // POOL_HALO
    n_groups = len(POOL_WINDOWS)
    return pl.pallas_call(
        functools.partial(_mixout_kernel, tiles_per_seq),
        grid=(t // tm,),
        in_specs=[
            pl.BlockSpec((tm, D_MODEL), lambda i: (i, 0)),
            pl.BlockSpec((tm, ATTN_WIDTH), lambda i: (i, 0)),
            pl.BlockSpec((tm, POOL_WIDTH), lambda i: (i, 0)),
            pl.BlockSpec((POOL_HALO, POOL_WIDTH), lambda i: (jnp.maximum(i * halo_blocks - 1, 0), 0)),
            pl.BlockSpec((None, n_groups, POOL_GROUP, POOL_GROUP), lambda i: (layer, 0, 0, 0)),
            pl.BlockSpec((None, 1, POOL_WIDTH), lambda i: (layer, 0, 0)),
            pl.BlockSpec((None, ATTN_WIDTH, D_MODEL), lambda i: (layer, 0, 0)),
            pl.BlockSpec((None, POOL_WIDTH, D_MODEL), lambda i: (layer, 1, 0)),
            pl.BlockSpec((None, 1, D_MODEL), lambda i: (layer, 0, 0)),
        ],
        out_specs=pl.BlockSpec((tm, D_MODEL), lambda i: (i, 0)),
        out_shape=jax.ShapeDtypeStruct((t, D_MODEL), F32),
        scratch_shapes=[pltpu.VMEM((POOL_PAD + POOL_HALO + tm, POOL_WIDTH), F32),
                        pltpu.VMEM((tm, POOL_WIDTH), BF16),
                        pltpu.VMEM((tm, D_MODEL), F32),
                        pltpu.VMEM((POOL_PAD + POOL_HALO + tm, POOL_GROUP), F32),
                        pltpu.VMEM((POOL_PAD + POOL_HALO + tm, POOL_GROUP), F32)],
        compiler_params=pltpu.CompilerParams(
            dimension_semantics=("arbitrary",),
            vmem_limit_bytes=VMEM_LIMIT_BYTES),
        name="mixout",
    )(x, a, u, u, pool_w, pool_scale, w_o, w_o, g)


def _gelu_tanh(x):
    return 0.5 * x * (1.0 + jnp.tanh(math.sqrt(2.0 / math.pi) * (x + 0.044715 * (x * x * x))))


def _ffn_kernel(tiles_per_seq, x_ref, g_ref, wg_ref, wv_ref, cw_ref, cb_ref,
                wd_ref, pg_ref, o_ref, h_ref, ug_ref, uv_ref, carry_g_ref, carry_v_ref):
    tm = FFN_TM
    tf = FFN_TF
    i = pl.program_id(0)
    j = pl.program_id(1)
    first_in_seq = (i % tiles_per_seq) == 0
    y_ref = o_ref

    @pl.when(j == 0)
    def _():
        x = x_ref[...]
        h_ref[...] = ((x * _rms_scale(x)) * g_ref[...]).astype(BF16)

    @pl.when(jnp.logical_and(i == 0, j == 0))
    def _():
        y_ref[...] = jnp.zeros_like(y_ref)

    sub = tm // FFN_ROW_SPLIT

    def up(r, w_ref, buf_ref):
        rows = slice(r * sub, (r + 1) * sub)
        buf_ref[CONV_HALO + r * sub:CONV_HALO + (r + 1) * sub, :] = jnp.dot(
            h_ref[rows, :], w_ref[...], preferred_element_type=F32)

    gate_cols = pl.ds(pl.multiple_of(j * tf, tf), tf)
    val_cols = pl.ds(pl.multiple_of(D_FF + j * tf, tf), tf)

    def conv(r, cols, buf_ref):
        out = cb_ref[:, cols]
        for tap in range(CONV_WIDTH):
            start = CONV_HALO - (CONV_WIDTH - 1) + tap + r * sub
            out = out + buf_ref[start:start + sub, :] * cw_ref[tap:tap + 1, cols]
        return out

    ug_ref[0:CONV_HALO, :] = jnp.where(first_in_seq, 0.0, carry_g_ref[j])
    uv_ref[0:CONV_HALO, :] = jnp.where(first_in_seq, 0.0, carry_v_ref[j])
    for r in range(FFN_ROW_SPLIT):
        up(r, wg_ref, ug_ref)
        up(r, wv_ref, uv_ref)
    carry_g_ref[j] = ug_ref[tm:tm + CONV_HALO, :]
    carry_v_ref[j] = uv_ref[tm:tm + CONV_HALO, :]
    for r in range(FFN_ROW_SPLIT):
        rows = slice(r * sub, (r + 1) * sub)
        gate = conv(r, gate_cols, ug_ref)
        val = conv(r, val_cols, uv_ref)
        act = (_gelu_tanh(gate) * val).astype(BF16)
        acc = jnp.where(j == 0, 0.0, y_ref[rows, :])
        y_ref[rows, :] = acc + jnp.dot(act, wd_ref[...], preferred_element_type=F32)

    @pl.when(j == pl.num_programs(1) - 1)
    def _():
        for c in range(tm // NORM_ROWS):
            rows = slice(c * NORM_ROWS, (c + 1) * NORM_ROWS)
            yy = y_ref[rows, :]
            o_ref[rows, :] = x_ref[rows, :] + (yy * _rms_scale(yy)) * pg_ref[...]


def _ffn(layer, x, g, w_up, conv_w, conv_b, w_down, post_g, seq_len):
    t = x.shape[0]
    tm, tf = FFN_TM, FFN_TF
    n_f = D_FF // tf
    tiles_per_seq = seq_len // tm
    return pl.pallas_call(
        functools.partial(_ffn_kernel, tiles_per_seq),
        grid=(t // tm, n_f),
        in_specs=[
            pl.BlockSpec((tm, D_MODEL), lambda i, j: (i, 0)),
            pl.BlockSpec((None, 1, D_MODEL), lambda i, j: (layer, 0, 0)),
            pl.BlockSpec((None, D_MODEL, tf), lambda i, j: (layer, 0, j)),
            pl.BlockSpec((None, D_MODEL, tf), lambda i, j: (layer, 0, j + n_f)),
            pl.BlockSpec((None, CONV_WIDTH, 2 * D_FF), lambda i, j: (layer, 0, 0)),
            pl.BlockSpec((None, 1, 2 * D_FF), lambda i, j: (layer, 0, 0)),
            pl.BlockSpec((None, tf, D_MODEL), lambda i, j: (layer, j, 0)),
            pl.BlockSpec((None, 1, D_MODEL), lambda i, j: (layer, 0, 0)),
        ],
        out_specs=pl.BlockSpec((tm, D_MODEL), lambda i, j: (i, 0)),
        out_shape=jax.ShapeDtypeStruct((t, D_MODEL), F32),
        scratch_shapes=[
            pltpu.VMEM((tm, D_MODEL), BF16),
            pltpu.VMEM((tm + CONV_HALO, tf), F32),
            pltpu.VMEM((tm + CONV_HALO, tf), F32),
            pltpu.VMEM((n_f, CONV_HALO, tf), F32),
            pltpu.VMEM((n_f, CONV_HALO, tf), F32),
        ],
        compiler_params=pltpu.CompilerParams(
            dimension_semantics=("arbitrary", "arbitrary"),
            vmem_limit_bytes=VMEM_LIMIT_BYTES),
        name="ffn",
    )(x, g, w_up, w_up, conv_w, conv_b, w_down, post_g)


def kernel(x, pre_mix_g, w_in, rel_bias, pool_w, pool_scale, w_o, post_mix_g,
           pre_ffn_g, w_up, conv_w, conv_b, w_down, post_ffn_g):
    b, s, d = x.shape
    depth = w_in.shape[0]
    assert d == D_MODEL and w_in.shape[1:] == (D_MODEL, IN_WIDTH) and w_up.shape[1:] == (D_MODEL, 2 * D_FF)
    assert all(s % tile == 0 for tile in (IN_TM, ATT_BQ, MIX_TM, FFN_TM)) and s // ATT_BQ >= ATT_NKB
    w_in, pool_w, w_o, w_up, w_down = (w.astype(BF16) for w in (w_in, pool_w, w_o, w_up, w_down))
    pre_mix_g, pool_scale, post_mix_g, pre_ffn_g, conv_b, post_ffn_g = (
        _row(v) for v in (pre_mix_g, pool_scale, post_mix_g, pre_ffn_g, conv_b, post_ffn_g))
    xt = x.reshape(b * s, d)
    for l in range(depth):
        z, u = _inproj(l, xt, pre_mix_g, w_in)
        a = _attention(z, _band_bias(rel_bias[l]), s)
        xt = _mixout(l, xt, a, u, pool_w, pool_scale, w_o, post_mix_g, s)
        xt = _ffn(l, xt, pre_ffn_g, w_up, conv_w, conv_b, w_down, post_ffn_g, s)
    return xt.reshape(b, s, d)
```

```python
import functools
import math

import jax
import jax.numpy as jnp
from jax import lax
from jax.experimental import pallas as pl
from jax.experimental.pallas import tpu as pltpu

F32 = jnp.float32
BF16 = jnp.bfloat16

D_MODEL = 2048
CHUNK = 64
LEFT_CHUNKS = 8
ATTN_WIDTH = 1024
POOL_WIDTH = 1024
HEAD_DIM = 128
N_HEADS = ATTN_WIDTH // HEAD_DIM
REL_CLIP = 128
POOL_WINDOWS = (2, 4, 8, 16)
POOL_GROUP = POOL_WIDTH // len(POOL_WINDOWS)
IN_WIDTH = 3 * ATTN_WIDTH + POOL_WIDTH
D_FF = 5632
CONV_WIDTH = 3
NORM_EPS = 1e-6
MASK_VALUE = -1e30
LOG2_E = math.log2(math.e)
QK_SCALE = LOG2_E / math.sqrt(HEAD_DIM)
Q_COL, K_COL, V_COL = 0, ATTN_WIDTH, 2 * ATTN_WIDTH

V7X_VMEM_BYTES = 64 * 1024 * 1024
VMEM_RESERVE_BYTES = 1024 * 1024
VMEM_LIMIT_BYTES = V7X_VMEM_BYTES - VMEM_RESERVE_BYTES

IN_TM = 1024
IN_TN = 2048
ATT_BQ = 4 * CHUNK
ATT_NKB = LEFT_CHUNKS * CHUNK // ATT_BQ + 1
ATT_BK = ATT_NKB * ATT_BQ
ATT_HALF = 2 * CHUNK
ATT_BAND = (LEFT_CHUNKS + 1) * CHUNK + ATT_HALF - CHUNK
MIX_TM = 512
MIX_ROW_SPLIT = 2
POOL_HALO = 16
POOL_PAD = 8
FFN_TM = 1024
FFN_TF = 512
FFN_ROW_SPLIT = 2
CONV_HALO = 8
NORM_ROWS = 16


def _rms_scale(x):
    return lax.rsqrt(jnp.mean(x * x, axis=-1, keepdims=True) + NORM_EPS)


def _row(vec):
    return vec.reshape(vec.shape[0], 1, vec.shape[1])


def _inproj_kernel(x_ref, w_ref, z_ref, u_ref, h_ref):
    j = pl.program_id(1)
    half = IN_TM // 2

    def normed(rows):
        x = x_ref[rows, :]
        return (x * _rms_scale(x)).astype(BF16)

    @pl.when(j == 0)
    def _():
        h_ref[0:half, :] = normed(slice(0, half))

    h_ref[half:, :] = normed(slice(half, IN_TM))
    q_scale = jnp.where(j == 0, QK_SCALE, 1.0).astype(F32)
    for rows in (slice(0, half), slice(half, IN_TM)):
        z = jnp.dot(h_ref[rows, :], w_ref[...], preferred_element_type=F32)
        z_ref[rows, :ATTN_WIDTH] = (z[:, :ATTN_WIDTH] * q_scale).astype(BF16)
        z_ref[rows, ATTN_WIDTH:] = z[:, ATTN_WIDTH:].astype(BF16)
        u_ref[rows, :] = z[:, IN_TN - POOL_WIDTH:]


def _inproj(layer, x, w):
    t = x.shape[0]
    grid = (t // IN_TM, IN_WIDTH // IN_TN)
    return pl.pallas_call(
        _inproj_kernel,
        grid=grid,
        in_specs=[
            pl.BlockSpec((IN_TM, D_MODEL), lambda i, j: (i, 0)),
            pl.BlockSpec((None, D_MODEL, IN_TN), lambda i, j: (layer, 0, j)),
        ],
        out_specs=[
            pl.BlockSpec((IN_TM, IN_TN), lambda i, j: (i, j)),
            pl.BlockSpec((IN_TM, POOL_WIDTH), lambda i, j: (i, 0)),
        ],
        out_shape=[
            jax.ShapeDtypeStruct((t, IN_WIDTH), BF16),
            jax.ShapeDtypeStruct((t, POOL_WIDTH), F32),
        ],
        scratch_shapes=[pltpu.VMEM((IN_TM, D_MODEL), BF16)],
        compiler_params=pltpu.CompilerParams(
            dimension_semantics=("arbitrary", "arbitrary"),
            vmem_limit_bytes=VMEM_LIMIT_BYTES),
        name="inproj",
    )(x, w)


def _band_bias(table):
    n_heads = table.shape[0]
    band_keys = (LEFT_CHUNKS + 1) * CHUNK
    far = band_keys - REL_CLIP
    near = REL_CLIP + CHUNK - 1
    table = table.astype(F32)
    ext = jnp.concatenate(
        [jnp.broadcast_to(table[:, 2 * REL_CLIP:], (n_heads, far)),
         table[:, 2 * REL_CLIP - near:2 * REL_CLIP][:, ::-1]], axis=1)
    rows = jnp.stack([ext[:, CHUNK - 1 - qi:CHUNK - 1 - qi + band_keys] for qi in range(CHUNK)], axis=1)
    rows = rows * LOG2_E
    pad = ATT_BK - band_keys
    masked = jnp.full((n_heads, CHUNK, pad), MASK_VALUE, F32)
    blocks = [jnp.concatenate([masked[:, :, :c * CHUNK], rows, masked[:, :, :pad - c * CHUNK]], axis=2)
              for c in range(ATT_BQ // CHUNK)]
    bias = jnp.concatenate(blocks, axis=1)
    key_block = lax.broadcasted_iota(jnp.int32, (ATT_NKB, 1, 1, ATT_BK), 3) // ATT_BQ
    n_masked = lax.broadcasted_iota(jnp.int32, (ATT_NKB, 1, 1, ATT_BK), 0)
    return jnp.where(key_block < n_masked, F32(MASK_VALUE), bias[None])


def _attn_kernel(kvq_ref, bias_ref, o_ref, kt_ref, v_ref, p_ref, linv_ref):
    @pl.when(pl.program_id(0) == 0)
    def _():
        kt_ref[...] = jnp.zeros_like(kt_ref)
        v_ref[...] = jnp.zeros_like(v_ref)
        p_ref[...] = jnp.zeros_like(p_ref)
        linv_ref[...] = jnp.zeros_like(linv_ref)

    v_ref[0:ATT_BK, :] = v_ref[ATT_BQ:ATT_BK + ATT_BQ, :]
    v_ref[ATT_BK:ATT_BK + ATT_BQ, :] = kvq_ref[:, V_COL:V_COL + ATTN_WIDTH]
    kt_ref[:, 0:ATT_BK] = kt_ref[:, ATT_BQ:ATT_BK + ATT_BQ]
    kt_ref[:, ATT_BK:ATT_BK + ATT_BQ] = kvq_ref[:, K_COL:K_COL + ATTN_WIDTH].T

    for h in range(0, N_HEADS, 2):
        pair = p_ref[h:h + 2].reshape(2 * ATT_BQ, ATT_BK)
        o = jnp.dot(pair, v_ref[0:ATT_BK, h * HEAD_DIM:(h + 2) * HEAD_DIM], preferred_element_type=F32)
        for k in range(2):
            hs = slice((h + k) * HEAD_DIM, (h + k + 1) * HEAD_DIM)
            o_k = o[k * ATT_BQ:(k + 1) * ATT_BQ, k * HEAD_DIM:(k + 1) * HEAD_DIM]
            o_ref[:, hs] = (o_k * linv_ref[h + k]).astype(BF16)

    for h in range(N_HEADS):
        hs = slice(h * HEAD_DIM, (h + 1) * HEAD_DIM)
        qs = slice(Q_COL + h * HEAD_DIM, Q_COL + (h + 1) * HEAD_DIM)
        s = jnp.dot(kvq_ref[:, qs], kt_ref[hs, ATT_BQ:ATT_BK + ATT_BQ],
                    preferred_element_type=F32)
        for half in range(ATT_BQ // ATT_HALF):
            rows = slice(half * ATT_HALF, (half + 1) * ATT_HALF)
            cols = slice(half * ATT_HALF, half * ATT_HALF + ATT_BAND)
            sh = s[rows, cols] + bias_ref[h, rows, cols]
            m = jnp.max(sh, axis=-1, keepdims=True)
            p = jnp.exp2(sh - m)
            l = jnp.sum(p, axis=-1, keepdims=True)
            p_ref[h, rows, cols] = p.astype(BF16)
            linv_ref[h, rows, :] = jnp.broadcast_to(1.0 / l, (ATT_HALF, HEAD_DIM))


def _attention(z, bias, seq_len):
    t = z.shape[0]
    n_blocks = t // ATT_BQ
    blocks_per_seq = seq_len // ATT_BQ

    def cur_block(i):
        return jnp.minimum(i, n_blocks - 1)

    def bias_map(i):
        return (jnp.maximum(ATT_NKB - 1 - cur_block(i) % blocks_per_seq, 0), 0, 0, 0)

    return pl.pallas_call(
        _attn_kernel,
        grid=(n_blocks + 1,),
        in_specs=[pl.BlockSpec((ATT_BQ, 3 * ATTN_WIDTH), lambda i: (cur_block(i), 0)),
                  pl.BlockSpec((None, N_HEADS, ATT_BQ, ATT_BK), bias_map)],
        out_specs=pl.BlockSpec((ATT_BQ, ATTN_WIDTH), lambda i: (jnp.maximum(i - 1, 0), 0)),
        out_shape=jax.ShapeDtypeStruct((t, ATTN_WIDTH), BF16),
        scratch_shapes=[pltpu.VMEM((ATTN_WIDTH, ATT_BK + ATT_BQ), BF16),
                        pltpu.VMEM((ATT_BK + ATT_BQ, ATTN_WIDTH), BF16),
                        pltpu.VMEM((N_HEADS, ATT_BQ, ATT_BK), BF16),
                        pltpu.VMEM((N_HEADS, ATT_BQ, HEAD_DIM), F32)],
        compiler_params=pltpu.CompilerParams(
            dimension_semantics=("arbitrary",),
            vmem_limit_bytes=VMEM_LIMIT_BYTES),
        name="attention",
    )(z, bias)


def _mixout_kernel(tiles_per_seq, x_ref, a_ref, u_ref, uh_ref, pw_ref, ps_ref, woa_ref, wop_ref,
                   g_ref, o_ref, ebuf_ref, p_ref, m_ref, sa_ref, sb_ref):
    tm = MIX_TM
    sub = tm // MIX_ROW_SPLIT
    i_in_seq = pl.program_id(0) % tiles_per_seq
    ext = POOL_HALO + tm
    own = POOL_PAD + POOL_HALO

    @pl.when(pl.program_id(0) == 0)
    def _():
        ebuf_ref[0:POOL_PAD, :] = jnp.zeros((POOL_PAD, POOL_WIDTH), F32)
        sa_ref[0:POOL_PAD, :] = jnp.zeros((POOL_PAD, POOL_GROUP), F32)
        sb_ref[0:POOL_PAD, :] = jnp.zeros((POOL_PAD, POOL_GROUP), F32)

    ebuf_ref[POOL_PAD:own, :] = jnp.where(i_in_seq == 0, 0.0, uh_ref[...])
    ebuf_ref[own:, :] = u_ref[...]

    def window_sum(w, cs):
        src, cols, span, spare = ebuf_ref, cs, 1, [sa_ref, sb_ref]
        while 2 * span < w:
            dst = spare.pop(0)
            dst[POOL_PAD:POOL_PAD + ext, :] = (src[POOL_PAD:POOL_PAD + ext, cols]
                                               + src[POOL_PAD - span:POOL_PAD - span + ext, cols])
            if src is not ebuf_ref:
                spare.append(src)
            src, cols, span = dst, slice(None), 2 * span
        return src[own:own + tm, cols] + src[own - span:own - span + tm, cols]

    n_groups = len(POOL_WINDOWS)
    a_cols = D_MODEL // n_groups
    head_pos = i_in_seq * tm + lax.broadcasted_iota(jnp.int32, (POOL_HALO, POOL_GROUP), 0)

    def attn_cols(c):
        cols = slice(c * a_cols, (c + 1) * a_cols)
        m_ref[:, cols] = jnp.dot(a_ref[...], woa_ref[:, cols], preferred_element_type=F32)

    for g, w in enumerate(POOL_WINDOWS):
        if g > 0:
            attn_cols(g - 1)
        if g == n_groups - 1:
            attn_cols(g)

        cs = slice(g * POOL_GROUP, (g + 1) * POOL_GROUP)
        tok = ebuf_ref[own:own + tm, cs]
        wsum = window_sum(w, cs)
        body = wsum * (1.0 / w) - tok
        head_count = jnp.minimum(head_pos + 1, w).astype(F32)
        head = wsum[:POOL_HALO] / head_count - tok[:POOL_HALO]
        pooled = jnp.concatenate([head, body[POOL_HALO:]], axis=0).astype(BF16)
        mixed = jnp.dot(pooled, pw_ref[g], preferred_element_type=F32)
        p_ref[:, cs] = (mixed * ps_ref[:, cs]).astype(BF16)

    for r in range(MIX_ROW_SPLIT):
        rows = slice(r * sub, (r + 1) * sub)
        m = m_ref[rows, :] + jnp.dot(p_ref[rows, :], wop_ref[...], preferred_element_type=F32)
        o_ref[rows, :] = x_ref[rows, :] + (m * _rms_scale(m)) * g_ref[...]


def _mixout(layer, x, a, u, pool_w, pool_scale, w_o, g, seq_len):
    t = x.shape[0]
    tm = MIX_TM
    tiles_per_seq = seq_len // tm
    halo_blocks = tm // POOL_HALO
    n_groups = len(POOL_WINDOWS)
    return pl.pallas_call(
        functools.partial(_mixout_kernel, tiles_per_seq),
        grid=(t // tm,),
        in_specs=[
            pl.BlockSpec((tm, D_MODEL), lambda i: (i, 0)),
            pl.BlockSpec((tm, ATTN_WIDTH), lambda i: (i, 0)),
            pl.BlockSpec((tm, POOL_WIDTH), lambda i: (i, 0)),
            pl.BlockSpec((POOL_HALO, POOL_WIDTH), lambda i: (jnp.maximum(i * halo_blocks - 1, 0), 0)),
            pl.BlockSpec((None, n_groups, POOL_GROUP, POOL_GROUP), lambda i: (layer, 0, 0, 0)),
            pl.BlockSpec((None, 1, POOL_WIDTH), lambda i: (layer, 0, 0)),
            pl.BlockSpec((None, ATTN_WIDTH, D_MODEL), lambda i: (layer, 0, 0)),
            pl.BlockSpec((None, POOL_WIDTH, D_MODEL), lambda i: (layer, 1, 0)),
            pl.BlockSpec((None, 1, D_MODEL), lambda i: (layer, 0, 0)),
        ],
        out_specs=pl.BlockSpec((tm, D_MODEL), lambda i: (i, 0)),
        out_shape=jax.ShapeDtypeStruct((t, D_MODEL), F32),
        scratch_shapes=[pltpu.VMEM((POOL_PAD + POOL_HALO + tm, POOL_WIDTH), F32),
                        pltpu.VMEM((tm, POOL_WIDTH), BF16),
                        pltpu.VMEM((tm, D_MODEL), F32),
                        pltpu.VMEM((POOL_PAD + POOL_HALO + tm, POOL_GROUP), F32),
                        pltpu.VMEM((POOL_PAD + POOL_HALO + tm, POOL_GROUP), F32)],
        compiler_params=pltpu.CompilerParams(
            dimension_semantics=("arbitrary",),
            vmem_limit_bytes=VMEM_LIMIT_BYTES),
        name="mixout",
    )(x, a, u, u, pool_w, pool_scale, w_o, w_o, g)


def _gelu_tanh(x):
    return 0.5 * x * (1.0 + jnp.tanh(math.sqrt(2.0 / math.pi) * (x + 0.044715 * (x * x * x))))


def _ffn_kernel(tiles_per_seq, x_ref, wg_ref, wv_ref, cw_ref, cb_ref,
                wd_ref, pg_ref, o_ref, h_ref, ug_ref, uv_ref, carry_g_ref, carry_v_ref):
    tm = FFN_TM
    tf = FFN_TF
    i = pl.program_id(0)
    j = pl.program_id(1)
    first_in_seq = (i % tiles_per_seq) == 0
    y_ref = o_ref

    @pl.when(j == 0)
    def _():
        x = x_ref[...]
        h_ref[...] = (x * _rms_scale(x)).astype(BF16)

    @pl.when(jnp.logical_and(i == 0, j == 0))
    def _():
        y_ref[...] = jnp.zeros_like(y_ref)

    sub = tm // FFN_ROW_SPLIT

    def up(r, w_ref, buf_ref):
        rows = slice(r * sub, (r + 1) * sub)
        buf_ref[CONV_HALO + r * sub:CONV_HALO + (r + 1) * sub, :] = jnp.dot(
            h_ref[rows, :], w_ref[...], preferred_element_type=F32)

    gate_cols = pl.ds(pl.multiple_of(j * tf, tf), tf)
    val_cols = pl.ds(pl.multiple_of(D_FF + j * tf, tf), tf)

    def conv(r, cols, buf_ref):
        out = cb_ref[:, cols]
        for tap in range(CONV_WIDTH):
            start = CONV_HALO - (CONV_WIDTH - 1) + tap + r * sub
            out = out + buf_ref[start:start + sub, :] * cw_ref[tap:tap + 1, cols]
        return out

    ug_ref[0:CONV_HALO, :] = jnp.where(first_in_seq, 0.0, carry_g_ref[j])
    uv_ref[0:CONV_HALO, :] = jnp.where(first_in_seq, 0.0, carry_v_ref[j])
    for r in range(FFN_ROW_SPLIT):
        up(r, wg_ref, ug_ref)
        up(r, wv_ref, uv_ref)
    carry_g_ref[j] = ug_ref[tm:tm + CONV_HALO, :]
    carry_v_ref[j] = uv_ref[tm:tm + CONV_HALO, :]
    for r in range(FFN_ROW_SPLIT):
        rows = slice(r * sub, (r + 1) * sub)
        gate = conv(r, gate_cols, ug_ref)
        val = conv(r, val_cols, uv_ref)
        act = (_gelu_tanh(gate) * val).astype(BF16)
        acc = jnp.where(j == 0, 0.0, y_ref[rows, :])
        y_ref[rows, :] = acc + jnp.dot(act, wd_ref[...], preferred_element_type=F32)

    @pl.when(j == pl.num_programs(1) - 1)
    def _():
        for c in range(tm // NORM_ROWS):
            rows = slice(c * NORM_ROWS, (c + 1) * NORM_ROWS)
            yy = y_ref[rows, :]
            o_ref[rows, :] = x_ref[rows, :] + (yy * _rms_scale(yy)) * pg_ref[...]


def _ffn(layer, x, w_up, conv_w, conv_b, w_down, post_g, seq_len):
    t = x.shape[0]
    tm, tf = FFN_TM, FFN_TF
    n_f = D_FF // tf
    tiles_per_seq = seq_len // tm
    return pl.pallas_call(
        functools.partial(_ffn_kernel, tiles_per_seq),
        grid=(t // tm, n_f),
        in_specs=[
            pl.BlockSpec((tm, D_MODEL), lambda i, j: (i, 0)),
            pl.BlockSpec((None, D_MODEL, tf), lambda i, j: (layer, 0, j)),
            pl.BlockSpec((None, D_MODEL, tf), lambda i, j: (layer, 0, j + n_f)),
            pl.BlockSpec((None, CONV_WIDTH, 2 * D_FF), lambda i, j: (layer, 0, 0)),
            pl.BlockSpec((None, 1, 2 * D_FF), lambda i, j: (layer, 0, 0)),
            pl.BlockSpec((None, tf, D_MODEL), lambda i, j: (layer, j, 0)),
            pl.BlockSpec((None, 1, D_MODEL), lambda i, j: (layer, 0, 0)),
        ],
        out_specs=pl.BlockSpec((tm, D_MODEL), lambda i, j: (i, 0)),
        out_shape=jax.ShapeDtypeStruct((t, D_MODEL), F32),
        scratch_shapes=[
            pltpu.VMEM((tm, D_MODEL), BF16),
            pltpu.VMEM((tm + CONV_HALO, tf), F32),
            pltpu.VMEM((tm + CONV_HALO, tf), F32),
            pltpu.VMEM((n_f, CONV_HALO, tf), F32),
            pltpu.VMEM((n_f, CONV_HALO, tf), F32),
        ],
        compiler_params=pltpu.CompilerParams(
            dimension_semantics=("arbitrary", "arbitrary"),
            vmem_limit_bytes=VMEM_LIMIT_BYTES),
        name="ffn",
    )(x, w_up, w_up, conv_w, conv_b, w_down, post_g)


def kernel(x, pre_mix_g, w_in, rel_bias, pool_w, pool_scale, w_o, post_mix_g,
           pre_ffn_g, w_up, conv_w, conv_b, w_down, post_ffn_g):
    b, s, d = x.shape
    depth = w_in.shape[0]
    assert d == D_MODEL and w_in.shape[1:] == (D_MODEL, IN_WIDTH) and w_up.shape[1:] == (D_MODEL, 2 * D_FF)
    assert all(s % tile == 0 for tile in (IN_TM, ATT_BQ, MIX_TM, FFN_TM)) and s // ATT_BQ >= ATT_NKB
    w_in = w_in * pre_mix_g[:, :, None]
    w_up = w_up * pre_ffn_g[:, :, None]
    w_in, pool_w, w_o, w_up, w_down = (w.astype(BF16) for w in (w_in, pool_w, w_o, w_up, w_down))
    pool_scale, post_mix_g, conv_b, post_ffn_g = (
        _row(v) for v in (pool_scale, post_mix_g, conv_b, post_ffn_g))
    xt = x.reshape(b * s, d)
    for l in range(depth):
        z, u = _inproj(l, xt, w_in)
        a = _attention(z, _band_bias(rel_bias[l]), s)
        xt = _mixout(l, xt, a, u, pool_w, pool_scale, w_o, post_mix_g, s)
        xt = _ffn(l, xt, w_up, conv_w, conv_b, w_down, post_ffn_g, s)
    return xt.reshape(b, s, d)
```
